```python
import math
import jax, jax.numpy as jnp
from jax import lax
import numpy as np

D_MODEL = 4096
BATCH = 4
SEQ = 2048
DEPTH = 1
DEC_BATCH = 128
DEC_SEQ = 8
PAST_LEN = 16384
PAGE_SIZE = 128

D_S5 = D_MODEL // 2
S5_GROUP = 16
S5_GROUPS = D_S5 // S5_GROUP
S5_STATE = 64
D_HGRN = D_MODEL // 2
HGRN_HEAD_DIM = 128
HGRN_HEADS = D_HGRN // HGRN_HEAD_DIM
HGRN_CHUNK = 64
D_IN = D_S5 + 4 * D_HGRN + 2 * D_MODEL
N_EXPERTS = 32
TOP_K = 4
D_EXPERT = D_MODEL
SWIGLU_LIMIT = 7.0
SWIGLU_ALPHA = 1.702
MOE_BLOCK = 128
RMS_EPS = 1e-5

kernel_name = "s5_hgrn2_gated_parallel_moe_decode_step"


def rms_norm(x, g):
    xf = x.astype(jnp.float32)
    y = xf * lax.rsqrt(jnp.mean(xf * xf, axis=-1, keepdims=True) + RMS_EPS)
    return (y * g.astype(jnp.float32)).astype(x.dtype)


def s5_branch(u, st_re, st_im, lam_re, lam_im, log_step, b_re, b_im, c_re, c_im, d_skip, w_glu):
    n, s, _ = u.shape
    ug = u.reshape(n, s, S5_GROUPS, S5_GROUP)
    dt = jnp.exp(log_step)[:, None]
    mag = jnp.exp(lam_re * dt)
    ang = lam_im * dt
    lbar_re, lbar_im = mag * jnp.cos(ang), mag * jnp.sin(ang)
    nr, ni = lbar_re - 1.0, lbar_im
    den = lam_re * lam_re + lam_im * lam_im
    fr = (nr * lam_re + ni * lam_im) / den
    fi = (ni * lam_re - nr * lam_im) / den
    bb_re = fr[:, :, None] * b_re - fi[:, :, None] * b_im
    bb_im = fr[:, :, None] * b_im + fi[:, :, None] * b_re
    bu_re = jnp.einsum('gph,nsgh->nsgp', bb_re, ug)
    bu_im = jnp.einsum('gph,nsgh->nsgp', bb_im, ug)
    bu_re = bu_re.at[:, 0].add(lbar_re * st_re - lbar_im * st_im)
    bu_im = bu_im.at[:, 0].add(lbar_re * st_im + lbar_im * st_re)
    a_re = jnp.broadcast_to(lbar_re, bu_re.shape)
    a_im = jnp.broadcast_to(lbar_im, bu_im.shape)

    def combine(e1, e2):
        a1r, a1i, b1r, b1i = e1
        a2r, a2i, b2r, b2i = e2
        return (a2r * a1r - a2i * a1i,
                a2r * a1i + a2i * a1r,
                a2r * b1r - a2i * b1i + b2r,
                a2r * b1i + a2i * b1r + b2i)

    _, _, xr, xi = lax.associative_scan(combine, (a_re, a_im, bu_re, bu_im), axis=1)
    y = (jnp.einsum('ghp,nsgp->nsgh', c_re, xr)
         - jnp.einsum('ghp,nsgp->nsgh', c_im, xi)
         + d_skip * ug).reshape(n, s, D_S5)
    y = jax.nn.gelu(y)
    out = y * jax.nn.sigmoid(y @ w_glu)
    return out, xr[:, -1], xi[:, -1]


def hgrn_branch(q, fz, iv, og, state, lb, norm_g):
    n, s, _ = q.shape
    dtype = q.dtype
    f32 = jnp.float32
    H, K = HGRN_HEADS, HGRN_HEAD_DIM
    lb = lb.astype(f32).reshape(H, K)
    fz = fz.astype(f32).reshape(n, s, H, K)
    log_f = jnp.log(lb + (1.0 - lb) * jax.nn.sigmoid(fz))
    k = (1.0 - lb) * jax.nn.sigmoid(-fz)
    qh = jax.nn.silu(q.astype(f32)).reshape(n, s, H, K) * (K ** -0.5)
    vh = iv.astype(f32).reshape(n, s, H, K)
    c = min(HGRN_CHUNK, s)
    nc = -(-s // c)
    pad = nc * c - s

    def to_chunks(t):
        t = jnp.pad(t, ((0, 0), (0, pad), (0, 0), (0, 0)))
        return t.reshape(n, nc, c, H, K).transpose(1, 0, 3, 2, 4)

    causal = jnp.tril(jnp.ones((c, c), dtype=bool))[:, :, None]

    def chunk_step(s_prev, inp):
        qc, kc, vc, gc = inp
        b = jnp.cumsum(gc, axis=2)
        diff = b[:, :, :, None, :] - b[:, :, None, :, :]
        decay = jnp.where(causal, jnp.exp(jnp.where(causal, diff, 0.0)), 0.0)
        scores = jnp.einsum('nhtk,nhsk,nhtsk->nhts', qc, kc, decay)
        o = (jnp.einsum('nhts,nhsv->nhtv', scores, vc)
             + jnp.einsum('nhtk,nhkv->nhtv', qc * jnp.exp(b), s_prev))
        b_end = b[:, :, -1]
        s_new = (jnp.exp(b_end)[..., None] * s_prev
                 + jnp.einsum('nhsk,nhsv->nhkv', kc * jnp.exp(b_end[:, :, None] - b), vc))
        return s_new, o

    s_fin, o = lax.scan(chunk_step, state.astype(f32),
                        (to_chunks(qh), to_chunks(k), to_chunks(vh), to_chunks(log_f)))
    o = o.transpose(1, 0, 3, 2, 4).reshape(n, nc * c, H, K)[:, :s]
    o = o * lax.rsqrt(jnp.mean(o * o, axis=-1, keepdims=True) + RMS_EPS) * norm_g.astype(f32)
    o = o.reshape(n, s, D_HGRN) * jax.nn.silu(og.astype(f32))
    return o.astype(dtype), s_fin.astype(dtype)


def moe(x, w_router, b_router, w_gate_up, b_gate_up, w_down, b_down):
    n, s, d = x.shape
    t = n * s
    xt = x.reshape(t, d)
    logits = (xt @ w_router + b_router).astype(jnp.float32)
    top_vals, top_idx = lax.top_k(logits, TOP_K)
    gates = jax.nn.softmax(top_vals, axis=-1)
    p = t * TOP_K
    flat_e = top_idx.reshape(p).astype(jnp.int32)
    flat_tok = jnp.arange(p, dtype=jnp.int32) // TOP_K
    order = jnp.argsort(flat_e)
    se, st, sg = flat_e[order], flat_tok[order], gates.reshape(p)[order]
    counts = jnp.bincount(flat_e, length=N_EXPERTS)
    starts = jnp.cumsum(counts) - counts
    pcounts = (counts + MOE_BLOCK - 1) // MOE_BLOCK * MOE_BLOCK
    pends = jnp.cumsum(pcounts)
    pstarts = pends - pcounts
    dest = pstarts[se] + jnp.arange(p, dtype=jnp.int32) - starts[se]
    n_blocks = -(-p // MOE_BLOCK) + N_EXPERTS
    rows = n_blocks * MOE_BLOCK
    slot_tok = jnp.full((rows,), t, dtype=jnp.int32).at[dest].set(st)
    xb = jnp.concatenate([xt, jnp.zeros((1, d), xt.dtype)], axis=0)[slot_tok]
    xb = xb.reshape(n_blocks, MOE_BLOCK, d)
    block_start = jnp.arange(n_blocks, dtype=jnp.int32) * MOE_BLOCK
    block_e = jnp.minimum(jnp.sum(pends[None, :] <= block_start[:, None], axis=1), N_EXPERTS - 1)

    def expert_block(args):
        xblk, e = args
        h = xblk @ w_gate_up[e] + b_gate_up[e]
        gate = jnp.minimum(h[:, ::2], SWIGLU_LIMIT)
        up = jnp.clip(h[:, 1::2], -SWIGLU_LIMIT, SWIGLU_LIMIT)
        glu = gate * jax.nn.sigmoid(SWIGLU_ALPHA * gate)
        return ((up + 1.0) * glu) @ w_down[e] + b_down[e]

    yb = lax.map(expert_block, (xb, block_e)).reshape(rows, d)
    y = yb[dest] * sg[:, None].astype(x.dtype)
    y = jax.ops.segment_sum(y, st, num_segments=t)
    return y.reshape(n, s, d)


def layer(x, s5_re, s5_im, hg_state, lb, norm_mix, w_in, b_gate, lam_re, lam_im, log_step,
          b_re, b_im, c_re, c_im, d_skip, w_glu, hgrn_norm, w_branch_s5, w_branch_hgrn, w_out,
          norm_ffn, w_router, b_router, w_gate_up, b_gate_up, w_down, b_down):
    h = rms_norm(x, norm_mix)
    z = h @ w_in
    o1 = D_S5
    o2 = o1 + D_HGRN
    o3 = o2 + D_HGRN
    o4 = o3 + D_HGRN
    o5 = o4 + D_HGRN
    u, q, fz, iv, og, gz = (z[..., :o1], z[..., o1:o2], z[..., o2:o3],
                            z[..., o3:o4], z[..., o4:o5], z[..., o5:])
    a, s5r, s5i = s5_branch(u, s5_re, s5_im, lam_re, lam_im, log_step, b_re, b_im,
                            c_re, c_im, d_skip, w_glu)
    b, s_new = hgrn_branch(q, fz, iv, og, hg_state, lb, hgrn_norm)
    gz = gz + b_gate
    g_a, g_b = gz[..., :D_MODEL], gz[..., D_MODEL:]
    merged = jax.nn.sigmoid(g_a) * (a @ w_branch_s5) + jax.nn.sigmoid(g_b) * (b @ w_branch_hgrn)
    x = x + merged @ w_out
    x = x + moe(rms_norm(x, norm_ffn), w_router, b_router, w_gate_up, b_gate_up, w_down, b_down)
    return x, s5r, s5i, s_new


def setup_inputs(seed: int = 0) -> dict:
    key = jax.random.key(seed)
    ks = jax.random.split(key, 40)
    f32 = jnp.float32
    nrm = lambda k, shape, scale: jax.random.normal(k, shape, f32) * scale
    G, P, H, K = S5_GROUPS, S5_STATE, S5_GROUP, HGRN_HEAD_DIM
    lam_im_base = jnp.pi * jnp.arange(P, dtype=f32)
    return {
        "x_prompt": nrm(ks[0], (BATCH, SEQ, D_MODEL), 1.0),
        "x_sample": nrm(ks[1], (DEC_BATCH, DEC_SEQ, D_MODEL), 1.0),
        "state_s5_re": nrm(ks[2], (DEPTH, DEC_BATCH, G, P), 0.1),
        "state_s5_im": nrm(ks[3], (DEPTH, DEC_BATCH, G, P), 0.1),
        "state_hgrn": nrm(ks[4], (DEPTH, DEC_BATCH, HGRN_HEADS, K, K), 0.5),
        "norm_mix": 1.0 + nrm(ks[5], (DEPTH, D_MODEL), 0.01),
        "w_in": nrm(ks[6], (DEPTH, D_MODEL, D_IN), D_MODEL ** -0.5),
        "b_gate": nrm(ks[7], (DEPTH, 2 * D_MODEL), 0.01),
        "s5_lam_re": -0.5 + nrm(ks[8], (DEPTH, G, P), 0.01),
        "s5_lam_im": lam_im_base + nrm(ks[9], (DEPTH, G, P), 0.01),
        "s5_log_step": jax.random.uniform(ks[10], (DEPTH, G), f32, math.log(1e-3), math.log(1e-1)),
        "s5_b_re": nrm(ks[11], (DEPTH, G, P, H), (2 * H) ** -0.5),
        "s5_b_im": nrm(ks[12], (DEPTH, G, P, H), (2 * H) ** -0.5),
        "s5_c_re": nrm(ks[13], (DEPTH, G, H, P), (2 * P) ** -0.5),
        "s5_c_im": nrm(ks[14], (DEPTH, G, H, P), (2 * P) ** -0.5),
        "s5_d": nrm(ks[15], (DEPTH, G, H), 1.0),
        "s5_w_glu": nrm(ks[16], (DEPTH, D_S5, D_S5), D_S5 ** -0.5),
        "hgrn_lb_logits": nrm(ks[17], (DEPTH + 1, D_HGRN), 1.0),
        "hgrn_norm": 1.0 + nrm(ks[18], (DEPTH, K), 0.01),
        "w_branch_s5": nrm(ks[19], (DEPTH, D_S5, D_MODEL), D_S5 ** -0.5),
        "w_branch_hgrn": nrm(ks[20], (DEPTH, D_HGRN, D_MODEL), D_HGRN ** -0.5),
        "w_out": nrm(ks[21], (DEPTH, D_MODEL, D_MODEL), D_MODEL ** -0.5),
        "norm_ffn": 1.0 + nrm(ks[22], (DEPTH, D_MODEL), 0.01),
        "w_router": nrm(ks[23], (DEPTH, D_MODEL, N_EXPERTS), D_MODEL ** -0.5),
        "b_router": nrm(ks[24], (DEPTH, N_EXPERTS), 0.01),
        "w_gate_up": nrm(ks[25], (DEPTH, N_EXPERTS, D_MODEL, 2 * D_EXPERT), D_MODEL ** -0.5),
        "b_gate_up": nrm(ks[26], (DEPTH, N_EXPERTS, 2 * D_EXPERT), 0.01),
        "w_down": nrm(ks[27], (DEPTH, N_EXPERTS, D_EXPERT, D_MODEL), D_EXPERT ** -0.5),
        "b_down": nrm(ks[28], (DEPTH, N_EXPERTS, D_MODEL), 0.01),
        "norm_final": 1.0 + nrm(ks[29], (D_MODEL,), 0.01),
    }


def reference(x_prompt, x_sample, state_s5_re, state_s5_im, state_hgrn, norm_mix, w_in, b_gate,
              s5_lam_re, s5_lam_im, s5_log_step, s5_b_re, s5_b_im, s5_c_re, s5_c_im, s5_d,
              s5_w_glu, hgrn_lb_logits, hgrn_norm, w_branch_s5, w_branch_hgrn, w_out, norm_ffn,
              w_router, b_router, w_gate_up, b_gate_up, w_down, b_down, norm_final):
    lb_all = jnp.cumsum(jax.nn.softmax(hgrn_lb_logits.astype(jnp.float32), axis=0), axis=0)[:DEPTH]
    dt = x_prompt.dtype
    xp, xs = x_prompt, x_sample
    p_re, p_im, p_hg, s_re, s_im, s_hg = [], [], [], [], [], []
    for l in range(DEPTH):
        weights = (lb_all[l], norm_mix[l], w_in[l], b_gate[l], s5_lam_re[l], s5_lam_im[l],
                   s5_log_step[l], s5_b_re[l], s5_b_im[l], s5_c_re[l], s5_c_im[l], s5_d[l],
                   s5_w_glu[l], hgrn_norm[l], w_branch_s5[l], w_branch_hgrn[l], w_out[l],
                   norm_ffn[l], w_router[l], b_router[l], w_gate_up[l], b_gate_up[l],
                   w_down[l], b_down[l])
        zs5 = jnp.zeros((BATCH, S5_GROUPS, S5_STATE), dt)
        zhg = jnp.zeros((BATCH, HGRN_HEADS, HGRN_HEAD_DIM, HGRN_HEAD_DIM), dt)
        xp, r, i, hg = layer(xp, zs5, zs5, zhg, *weights)
        p_re.append(r)
        p_im.append(i)
        p_hg.append(hg)
        xs, r, i, hg = layer(xs, state_s5_re[l], state_s5_im[l], state_hgrn[l], *weights)
        s_re.append(r)
        s_im.append(i)
        s_hg.append(hg)
    y_prompt = rms_norm(xp, norm_final)
    y_sample = rms_norm(xs, norm_final)
    return (y_prompt, y_sample, jnp.stack(p_re), jnp.stack(p_im), jnp.stack(p_hg),
            jnp.stack(s_re), jnp.stack(s_im), jnp.stack(s_hg))
```

```python
import functools

import jax
import jax.numpy as jnp
from jax import lax
from jax.experimental import pallas as pl
from jax.experimental.pallas import tpu as pltpu

F32 = jnp.float32
BF16 = jnp.bfloat16

RMS_EPS = 1e-5
S5_GROUP = 16
S5_STATE = 64
S5_GROUPS_PER_BLOCK = 16
HGRN_HEAD_DIM = 128
HGRN_CHUNK = 32
TOP_K = 4
SWIGLU_LIMIT = 7.0
SWIGLU_ALPHA = 1.702
EXP_CLAMP = 80.0

V7X_LANES = 128
V7X_BF16_SUBLANES = 16
V7X_VMEM_LIMIT = 56 * 1024 * 1024
CAST_ROWS = 256


def _cparams(semantics, vmem=V7X_VMEM_LIMIT):
    return pltpu.CompilerParams(dimension_semantics=semantics, vmem_limit_bytes=vmem)


def _sigmoid(x):
    return 1.0 / (1.0 + jnp.exp(-x))


def _cast_weight(w_ref, wbf_ref):
    k = w_ref.shape[0]
    rows = min(CAST_ROWS, k)

    def body(i, carry):
        r = pl.multiple_of(i * rows, rows)
        wbf_ref[pl.ds(r, rows), :] = w_ref[pl.ds(r, rows), :].astype(BF16)
        return carry

    lax.fori_loop(0, k // rows, body, 0)


def _rmsnorm_body(x_ref, g_ref, o_ref):
    x = x_ref[...]
    ms = jnp.mean(x * x, axis=-1, keepdims=True)
    o_ref[...] = (x * lax.rsqrt(ms + RMS_EPS) * g_ref[...]).astype(o_ref.dtype)


def _rmsnorm(x, g, out_dtype, tm=256):
    t, d = x.shape
    return pl.pallas_call(
        _rmsnorm_body,
        grid=(t // tm,),
        in_specs=[pl.BlockSpec((tm, d), lambda i: (i, 0)),
                  pl.BlockSpec((1, d), lambda i: (0, 0))],
        out_specs=pl.BlockSpec((tm, d), lambda i: (i, 0)),
        out_shape=jax.ShapeDtypeStruct((t, d), out_dtype),
        compiler_params=_cparams(("arbitrary",)),
        name="rmsnorm",
    )(x, g.reshape(1, d))


def _mm_body(*refs, n_x, n_extra, epilogue):
    x_refs = refs[:n_x]
    w_refs = refs[n_x:2 * n_x]
    extra_refs = refs[2 * n_x:2 * n_x + n_extra]
    o_ref = refs[2 * n_x + n_extra]
    wbf_refs = refs[2 * n_x + n_extra + 1:]

    @pl.when(pl.program_id(1) == 0)
    def _():
        for w_ref, wbf_ref in zip(w_refs, wbf_refs):
            _cast_weight(w_ref, wbf_ref)

    accs = [jnp.dot(x_ref[...].astype(BF16), wbf_ref[...], preferred_element_type=F32)
            for x_ref, wbf_ref in zip(x_refs, wbf_refs)]
    o_ref[...] = epilogue(accs, [e[...] for e in extra_refs]).astype(o_ref.dtype)


def _dense_mm(xs, ws, extras, epilogue, out_dtype, tm, tn, name):
    m_total = xs[0][0].shape[0]
    n_total = ws[0].shape[1]
    in_specs = []
    for (x, cb), w in zip(xs, ws):
        in_specs.append(pl.BlockSpec((tm, w.shape[0]), functools.partial(lambda n, m, cb: (m, cb), cb=cb)))
    for w in ws:
        in_specs.append(pl.BlockSpec((w.shape[0], tn), lambda n, m: (0, n)))
    for _, bshape, imap in extras:
        in_specs.append(pl.BlockSpec(bshape, imap))
    body = functools.partial(_mm_body, n_x=len(xs), n_extra=len(extras), epilogue=epilogue)
    return pl.pallas_call(
        body,
        grid=(n_total // tn, m_total // tm),
        in_specs=in_specs,
        out_specs=pl.BlockSpec((tm, tn), lambda n, m: (m, n)),
        out_shape=jax.ShapeDtypeStruct((m_total, n_total), out_dtype),
        scratch_shapes=[pltpu.VMEM((w.shape[0], tn), BF16) for w in ws],
        compiler_params=_cparams(("arbitrary", "arbitrary")),
        name=name,
    )(*[x for x, _ in xs], *ws, *[e for e, _, _ in extras])


def _s5_params(lam_re, lam_im, log_step, b_re, b_im, c_re, c_im):
    g, p = lam_re.shape
    h = b_re.shape[-1]
    gb = S5_GROUPS_PER_BLOCK
    nb = g // gb
    dt = jnp.exp(log_step)[:, None]
    mag = jnp.exp(lam_re * dt)
    ang = lam_im * dt
    lbar_re, lbar_im = mag * jnp.cos(ang), mag * jnp.sin(ang)
    nr, ni = lbar_re - 1.0, lbar_im
    den = lam_re * lam_re + lam_im * lam_im
    fr = (nr * lam_re + ni * lam_im) / den
    fi = (ni * lam_re - nr * lam_im) / den
    bb_re = fr[:, :, None] * b_re - fi[:, :, None] * b_im
    bb_im = fr[:, :, None] * b_im + fi[:, :, None] * b_re
    eye = jnp.eye(gb, dtype=F32)

    def blockdiag_in(bb):
        t = bb.reshape(nb, gb, p, h)
        return jnp.einsum('jgph,gk->jghkp', t, eye).reshape(nb, gb * h, gb * p)

    def blockdiag_out(c):
        t = c.reshape(nb, gb, h, p)
        return jnp.einsum('jghp,gk->jgpkh', t, eye).reshape(nb, gb * p, gb * h)

    wb = jnp.concatenate([blockdiag_in(bb_re), blockdiag_in(bb_im)], axis=2).astype(BF16)
    wc = jnp.concatenate([blockdiag_out(c_re), -blockdiag_out(c_im)], axis=1).astype(BF16)
    lam = jnp.concatenate([lbar_re.reshape(nb, 1, gb * p), lbar_im.reshape(nb, 1, gb * p)], axis=2)
    return wb, wc, lam


def _s5_body(u_ref, sre_ref, sim_ref, wb_ref, wc_ref, lam_ref, d_ref,
             y_ref, nre_ref, nim_ref, bu_ref, cre_ref, cim_ref, *, nsb, seq_rows, spb):
    hs = cre_ref.shape[1]
    u = u_ref[...]
    bu_ref[...] = jnp.dot(u.astype(BF16), wb_ref[...], preferred_element_type=F32)

    @pl.when(pl.program_id(1) % spb == 0)
    def _():
        cre_ref[...] = sre_ref[...]
        cim_ref[...] = sim_ref[...]

    lr = lam_ref[:, :hs]
    li = lam_ref[:, hs:]

    def seq_body(s, carry):
        def t_body(t, c):
            xr, xi = c
            row = s * seq_rows + t
            br = bu_ref[pl.ds(row, 1), :hs]
            bi = bu_ref[pl.ds(row, 1), hs:]
            nr = lr * xr - li * xi + br
            ni = lr * xi + li * xr + bi
            bu_ref[pl.ds(row, 1), :hs] = nr
            bu_ref[pl.ds(row, 1), hs:] = ni
            return nr, ni

        xr, xi = lax.fori_loop(0, seq_rows, t_body,
                               (cre_ref[pl.ds(s, 1), :], cim_ref[pl.ds(s, 1), :]), unroll=8)
        cre_ref[pl.ds(s, 1), :] = xr
        cim_ref[pl.ds(s, 1), :] = xi
        return carry

    lax.fori_loop(0, nsb, seq_body, 0)
    y = jnp.dot(bu_ref[...].astype(BF16), wc_ref[...], preferred_element_type=F32) + d_ref[...] * u
    y_ref[...] = jax.nn.gelu(y)
    nre_ref[...] = cre_ref[...]
    nim_ref[...] = cim_ref[...]


def _s5_scan(z, row0, n_seq, seq_len, st_re, st_im, wb, wc, lam, d_flat, rb_rows):
    nb, bw, sw = wb.shape
    hs = sw // 2
    if seq_len >= rb_rows:
        nsb, seq_rows, spb = 1, rb_rows, seq_len // rb_rows
    else:
        nsb, seq_rows, spb = rb_rows // seq_len, seq_len, 1
    n_rb = n_seq * seq_len // rb_rows
    n_sblk = n_seq // nsb
    rb0 = row0 // rb_rows
    st_re3 = st_re.reshape(n_sblk, nsb, nb * hs)
    st_im3 = st_im.reshape(n_sblk, nsb, nb * hs)
    body = functools.partial(_s5_body, nsb=nsb, seq_rows=seq_rows, spb=spb)
    state_spec = pl.BlockSpec((None, nsb, hs), lambda j, r: (r // spb, 0, j))
    y, nre, nim = pl.pallas_call(
        body,
        grid=(nb, n_rb),
        in_specs=[pl.BlockSpec((rb_rows, bw), lambda j, r: (rb0 + r, j)),
                  state_spec, state_spec,
                  pl.BlockSpec((None, bw, sw), lambda j, r: (j, 0, 0)),
                  pl.BlockSpec((None, sw, bw), lambda j, r: (j, 0, 0)),
                  pl.BlockSpec((None, 1, sw), lambda j, r: (j, 0, 0)),
                  pl.BlockSpec((1, bw), lambda j, r: (0, j))],
        out_specs=[pl.BlockSpec((rb_rows, bw), lambda j, r: (r, j)),
                   state_spec, state_spec],
        out_shape=[jax.ShapeDtypeStruct((n_seq * seq_len, nb * bw), F32),
                   jax.ShapeDtypeStruct((n_sblk, nsb, nb * hs), F32),
                   jax.ShapeDtypeStruct((n_sblk, nsb, nb * hs), F32)],
        scratch_shapes=[pltpu.VMEM((rb_rows, sw), F32),
                        pltpu.VMEM((nsb, hs), F32),
                        pltpu.VMEM((nsb, hs), F32)],
        compiler_params=_cparams(("arbitrary", "arbitrary")),
        name="s5_scan",
    )(z, st_re3, st_im3, wb, wc, lam, d_flat)
    return y, nre.reshape(n_seq, nb * hs), nim.reshape(n_seq, nb * hs)


def _cumsum_rows(x, seg):
    row = lax.broadcasted_iota(jnp.int32, x.shape, 0) % seg
    s = 1
    while s < seg:
        x = x + jnp.where(row >= s, pltpu.roll(x, s, axis=0), 0.0)
        s *= 2
    return x


def _hgrn_body(q_ref, f_ref, v_ref, g_ref, st_ref, lb_ref, ng_ref, o_ref, ns_ref, s_ref,
               *, rows, prows, n_chunks, n_heads):
    kd = HGRN_HEAD_DIM
    c = pl.program_id(1)

    @pl.when(c == 0)
    def _():
        s_ref[...] = st_ref[...]

    lb = lb_ref[...]
    fz = f_ref[...]
    q = q_ref[...]
    log_f = jnp.log(lb + (1.0 - lb) * _sigmoid(fz))
    kk = (1.0 - lb) * _sigmoid(-fz)
    qh = q * _sigmoid(q) * (kd ** -0.5)
    v = v_ref[...]
    if prows > rows:
        pad = jnp.zeros((prows - rows, fz.shape[1]), F32)
        log_f, kk, qh, v = [jnp.concatenate([a, pad], axis=0) for a in (log_f, kk, qh, v)]
    b = _cumsum_rows(log_f, prows)
    b_end = b[rows - 1:rows, :]
    b_mid = b[rows // 2 - 1:rows // 2, :]
    qt = (qh * jnp.exp(jnp.minimum(b - b_mid, EXP_CLAMP))).astype(BF16)
    kt = (kk * jnp.exp(jnp.minimum(b_mid - b, EXP_CLAMP))).astype(BF16)
    qe = (qh * jnp.exp(b)).astype(BF16)
    ke = (kk * jnp.exp(b_end - b)).astype(BF16)
    dec = jnp.exp(b_end)
    vb = v.astype(BF16)
    causal = (lax.broadcasted_iota(jnp.int32, (prows, prows), 0)
              >= lax.broadcasted_iota(jnp.int32, (prows, prows), 1))
    eye = (lax.broadcasted_iota(jnp.int32, (kd, kd), 0)
           == lax.broadcasted_iota(jnp.int32, (kd, kd), 1))
    og = g_ref[...]
    gate = og * _sigmoid(og)
    ng = ng_ref[...]
    for h in range(n_heads):
        sl = slice(h * kd, (h + 1) * kd)
        sc = lax.dot_general(qt[:, sl], kt[:, sl], (((1,), (1,)), ((), ())),
                             preferred_element_type=F32)
        sc = jnp.where(causal, sc, 0.0).astype(BF16)
        s_prev = s_ref[h]
        o = (jnp.dot(sc, vb[:, sl], preferred_element_type=F32)
             + jnp.dot(qe[:, sl], s_prev.astype(BF16), preferred_element_type=F32))
        upd = lax.dot_general(ke[:, sl], vb[:, sl], (((0,), (0,)), ((), ())),
                              preferred_element_type=F32)
        dec_col = jnp.sum(jnp.where(eye, jnp.broadcast_to(dec[:, sl], (kd, kd)), 0.0),
                          axis=1, keepdims=True)
        s_ref[h] = dec_col * s_prev + upd
        o = o[:rows]
        o = o * lax.rsqrt(jnp.mean(o * o, axis=-1, keepdims=True) + RMS_EPS) * ng
        o_ref[:, sl] = o * gate[:, sl]

    @pl.when(c == n_chunks - 1)
    def _():
        ns_ref[...] = s_ref[...]


def _hgrn(z, row0, n_seq, seq_len, state, lb_flat, norm_g, col_block0):
    n_heads, kd = state.shape[1], state.shape[2]
    dh = n_heads * kd
    rows = min(HGRN_CHUNK, seq_len)
    prows = max(rows, V7X_BF16_SUBLANES)
    n_chunks = seq_len // rows
    rb0 = row0 // rows
    body = functools.partial(_hgrn_body, rows=rows, prows=prows, n_chunks=n_chunks, n_heads=n_heads)

    def zspec(i):
        return pl.BlockSpec((rows, dh), lambda s, c: (rb0 + s * n_chunks + c, col_block0 + i))

    st_spec = pl.BlockSpec((None, n_heads, kd, kd), lambda s, c: (s, 0, 0, 0))
    o, ns = pl.pallas_call(
        body,
        grid=(n_seq, n_chunks),
        in_specs=[zspec(0), zspec(1), zspec(2), zspec(3), st_spec,
                  pl.BlockSpec((1, dh), lambda s, c: (0, 0)),
                  pl.BlockSpec((1, kd), lambda s, c: (0, 0))],
        out_specs=[pl.BlockSpec((rows, dh), lambda s, c: (s * n_chunks + c, 0)), st_spec],
        out_shape=[jax.ShapeDtypeStruct((n_seq * seq_len, dh), F32),
                   jax.ShapeDtypeStruct(state.shape, F32)],
        scratch_shapes=[pltpu.VMEM((n_heads, kd, kd), F32)],
        compiler_params=_cparams(("arbitrary", "arbitrary")),
        name="hgrn",
    )(z, z, z, z, state, lb_flat, norm_g.reshape(1, kd))
    return o, ns


def _router_body(x_ref, g_ref, w_ref, b_ref, xn_ref, idx_ref, gate_ref):
    x = x_ref[...]
    ms = jnp.mean(x * x, axis=-1, keepdims=True)
    xn = x * lax.rsqrt(ms + RMS_EPS) * g_ref[...]
    xn_ref[...] = xn
    logits = jnp.dot(xn.astype(BF16), w_ref[...].astype(BF16), preferred_element_type=F32) + b_ref[...]
    n_e = logits.shape[1]
    lane = lax.broadcasted_iota(jnp.int32, logits.shape, 1)
    out_lane = lax.broadcasted_iota(jnp.int32, idx_ref.shape, 1)
    idx_out = jnp.zeros(idx_ref.shape, jnp.int32)
    val_out = jnp.full(gate_ref.shape, -jnp.inf, F32)
    work = logits
    for k in range(TOP_K):
        m = jnp.max(work, axis=-1, keepdims=True)
        i = jnp.min(jnp.where(work == m, lane, n_e), axis=-1, keepdims=True)
        idx_out = jnp.where(out_lane == k, i, idx_out)
        val_out = jnp.where(out_lane == k, m, val_out)
        work = jnp.where(lane == i, -jnp.inf, work)
    top = jnp.max(val_out, axis=-1, keepdims=True)
    e = jnp.exp(val_out - top)
    gate_ref[...] = e / jnp.sum(e, axis=-1, keepdims=True)
    idx_ref[...] = idx_out


def _router(x, g, w_router, b_router, tm=256):
    t, d = x.shape
    n_e = w_router.shape[1]
    row = lambda i: (i, 0)
    fixed = lambda i: (0, 0)
    return pl.pallas_call(
        _router_body,
        grid=(t // tm,),
        in_specs=[pl.BlockSpec((tm, d), row), pl.BlockSpec((1, d), fixed),
                  pl.BlockSpec((d, n_e), fixed), pl.BlockSpec((1, n_e), fixed)],
        out_specs=[pl.BlockSpec((tm, d), row), pl.BlockSpec((tm, V7X_LANES), row),
                   pl.BlockSpec((tm, V7X_LANES), row)],
        out_shape=[jax.ShapeDtypeStruct((t, d), F32),
                   jax.ShapeDtypeStruct((t, V7X_LANES), jnp.int32),
                   jax.ShapeDtypeStruct((t, V7X_LANES), F32)],
        compiler_params=_cparams(("arbitrary",)),
        name="router",
    )(x, g.reshape(1, d), w_router, b_router.reshape(1, n_e))


def _gather_rows_copy(src_hbm, buf_ref, sem_ref, tok, slot, r):
    return pltpu.make_async_copy(src_hbm.at[pl.ds(tok, 1), :],
                                 buf_ref.at[slot, pl.ds(r, 1), :],
                                 sem_ref.at[slot])


def _gather_body(tok_ref, x_hbm, o_ref, buf_ref, sem_ref, *, tm):
    i = pl.program_id(0)
    n = pl.num_programs(0)

    def issue(block, slot):
        def body(r, carry):
            _gather_rows_copy(x_hbm, buf_ref, sem_ref, tok_ref[block * tm + r], slot, r).start()
            return carry
        lax.fori_loop(0, tm, body, 0)

    @pl.when(i == 0)
    def _():
        issue(0, 0)

    @pl.when(i + 1 < n)
    def _():
        issue(i + 1, (i + 1) % 2)

    slot = i % 2

    def wait_body(r, carry):
        _gather_rows_copy(x_hbm, buf_ref, sem_ref, 0, slot, r).wait()
        return carry

    lax.fori_loop(0, tm, wait_body, 0)
    o_ref[...] = buf_ref[slot].astype(o_ref.dtype)


def _gather_rows(x, tok, tm):
    r_total = tok.shape[0]
    d = x.shape[1]
    return pl.pallas_call(
        functools.partial(_gather_body, tm=tm),
        grid_spec=pltpu.PrefetchScalarGridSpec(
            num_scalar_prefetch=1,
            grid=(r_total // tm,),
            in_specs=[pl.BlockSpec(memory_space=pl.ANY)],
            out_specs=pl.BlockSpec((tm, d), lambda i, tok: (i, 0)),
            scratch_shapes=[pltpu.VMEM((2, tm, d), x.dtype), pltpu.SemaphoreType.DMA((2,))]),
        out_shape=jax.ShapeDtypeStruct((r_total, d), BF16),
        compiler_params=_cparams(("arbitrary",)),
        name="moe_gather",
    )(tok, x)


def _swiglu_epilogue(h, sel):
    tn = h.shape[1]
    gate = jnp.minimum(h, SWIGLU_LIMIT)
    up = jnp.clip(h, -SWIGLU_LIMIT, SWIGLU_LIMIT)
    glu = gate * _sigmoid(SWIGLU_ALPHA * gate)
    outs = []
    for c in range(tn // V7X_LANES):
        sl = slice(c * V7X_LANES, (c + 1) * V7X_LANES)
        up_next = pltpu.roll(up[:, sl], V7X_LANES - 1, axis=1)
        outs.append(((up_next + 1.0) * glu[:, sl]).astype(BF16))
    prod = jnp.concatenate(outs, axis=1)
    acts = []
    for c in range(tn // (2 * V7X_LANES)):
        sl = slice(c * 2 * V7X_LANES, (c + 1) * 2 * V7X_LANES)
        acts.append(jnp.dot(prod[:, sl], sel, preferred_element_type=F32).astype(BF16))
    return jnp.concatenate(acts, axis=1)


def _expert_mm_body(be_ref, first_ref, nvalid_ref, x_ref, w_ref, b_ref, *rest, swiglu):
    if swiglu:
        sel_ref, o_ref, wbf_ref = rest
    else:
        o_ref, wbf_ref = rest
    rb = pl.program_id(1)

    @pl.when((rb == 0) | (first_ref[rb] == 1))
    def _():
        _cast_weight(w_ref, wbf_ref)

    @pl.when(rb < nvalid_ref[0])
    def _():
        h = jnp.dot(x_ref[...], wbf_ref[...], preferred_element_type=F32) + b_ref[...]
        if swiglu:
            o_ref[...] = _swiglu_epilogue(h, sel_ref[...])
        else:
            o_ref[...] = h

    @pl.when(rb >= nvalid_ref[0])
    def _():
        o_ref[...] = jnp.zeros(o_ref.shape, o_ref.dtype)


def _expert_mm(x, w, bias, block_e, block_first, n_valid, tm, tn, swiglu):
    r_total, k = x.shape
    n_e, _, n_total = w.shape
    n_rb = r_total // tm
    out_tn = tn // 2 if swiglu else tn
    out_n = n_total // 2 if swiglu else n_total

    def last_valid(rb, nv):
        return jnp.minimum(rb, nv[0] - 1)

    in_specs = [pl.BlockSpec((tm, k), lambda n, rb, be, fi, nv: (last_valid(rb, nv), 0)),
                pl.BlockSpec((None, k, tn), lambda n, rb, be, fi, nv: (be[rb], 0, n)),
                pl.BlockSpec((None, 1, tn), lambda n, rb, be, fi, nv: (be[rb], 0, n))]
    args = [x, w, bias.reshape(n_e, 1, n_total)]
    if swiglu:
        sel = (lax.broadcasted_iota(jnp.int32, (2 * V7X_LANES, V7X_LANES), 0)
               == 2 * lax.broadcasted_iota(jnp.int32, (2 * V7X_LANES, V7X_LANES), 1)).astype(BF16)
        in_specs.append(pl.BlockSpec((2 * V7X_LANES, V7X_LANES), lambda n, rb, be, fi, nv: (0, 0)))
        args.append(sel)
    return pl.pallas_call(
        functools.partial(_expert_mm_body, swiglu=swiglu),
        grid_spec=pltpu.PrefetchScalarGridSpec(
            num_scalar_prefetch=3,
            grid=(n_total // tn, n_rb),
            in_specs=in_specs,
            out_specs=pl.BlockSpec((tm, out_tn), lambda n, rb, be, fi, nv: (rb, n)),
            scratch_shapes=[pltpu.VMEM((k, tn), BF16)]),
        out_shape=jax.ShapeDtypeStruct((r_total, out_n), BF16 if swiglu else F32),
        compiler_params=_cparams(("arbitrary", "arbitrary")),
        name="moe_gate_up" if swiglu else "moe_down",
    )(block_e, block_first, n_valid, *args)


def _combine_rows_copy(src_hbm, buf_ref, sem_ref, row, slot, k, r):
    return pltpu.make_async_copy(src_hbm.at[pl.ds(row, 1), :],
                                 buf_ref.at[slot, k, pl.ds(r, 1), :],
                                 sem_ref.at[slot])


def _combine_body(dest_ref, yb_hbm, x_ref, gate_ref, g_ref, o_ref, buf_ref, sem_ref, *, tm):
    i = pl.program_id(0)
    n = pl.num_programs(0)

    def issue(block, slot):
        def body(r, carry):
            for k in range(TOP_K):
                row = dest_ref[(block * tm + r) * TOP_K + k]
                _combine_rows_copy(yb_hbm, buf_ref, sem_ref, row, slot, k, r).start()
            return carry
        lax.fori_loop(0, tm, body, 0)

    @pl.when(i == 0)
    def _():
        issue(0, 0)

    @pl.when(i + 1 < n)
    def _():
        issue(i + 1, (i + 1) % 2)

    slot = i % 2

    def wait_body(r, carry):
        for k in range(TOP_K):
            _combine_rows_copy(yb_hbm, buf_ref, sem_ref, 0, slot, k, r).wait()
        return carry

    lax.fori_loop(0, tm, wait_body, 0)
    gates = gate_ref[...]
    acc = x_ref[...]
    for k in range(TOP_K):
        acc = acc + gates[:, k:k + 1] * buf_ref[slot, k]
    ms = jnp.mean(acc * acc, axis=-1, keepdims=True)
    o_ref[...] = acc * lax.rsqrt(ms + RMS_EPS) * g_ref[...]


def _combine(yb, dest_flat, x, gates, g, tm):
    t, d = x.shape
    return pl.pallas_call(
        functools.partial(_combine_body, tm=tm),
        grid_spec=pltpu.PrefetchScalarGridSpec(
            num_scalar_prefetch=1,
            grid=(t // tm,),
            in_specs=[pl.BlockSpec(memory_space=pl.ANY),
                      pl.BlockSpec((tm, d), lambda i, dest: (i, 0)),
                      pl.BlockSpec((tm, V7X_LANES), lambda i, dest: (i, 0)),
                      pl.BlockSpec((1, d), lambda i, dest: (0, 0))],
            out_specs=pl.BlockSpec((tm, d), lambda i, dest: (i, 0)),
            scratch_shapes=[pltpu.VMEM((2, TOP_K, tm, d), F32), pltpu.SemaphoreType.DMA((2,))]),
        out_shape=jax.ShapeDtypeStruct((t, d), F32),
        compiler_params=_cparams(("arbitrary",)),
        name="moe_combine",
    )(dest_flat, yb, x, gates, g.reshape(1, d))


def _routing_tables(top_idx, n_e, tm):
    t = top_idx.shape[0]
    p = t * TOP_K
    flat_e = top_idx.reshape(p)
    onehot = (flat_e[:, None] == jnp.arange(n_e, dtype=jnp.int32)[None, :]).astype(jnp.int32)
    rank = jnp.take_along_axis(jnp.cumsum(onehot, axis=0), flat_e[:, None], axis=1)[:, 0] - 1
    counts = jnp.sum(onehot, axis=0)
    pcounts = (counts + tm - 1) // tm * tm
    pends = jnp.cumsum(pcounts)
    pstarts = pends - pcounts
    dest = (pstarts[flat_e] + rank).astype(jnp.int32)
    n_blocks = p // tm + n_e
    flat_tok = jnp.arange(p, dtype=jnp.int32) // TOP_K
    slot_tok = jnp.zeros((n_blocks * tm,), jnp.int32).at[dest].set(flat_tok)
    block_start = jnp.arange(n_blocks, dtype=jnp.int32) * tm
    block_e = jnp.minimum(jnp.sum(pends[None, :] <= block_start[:, None], axis=1), n_e - 1).astype(jnp.int32)
    n_valid = (pends[-1] // tm).astype(jnp.int32).reshape(1)
    last_e = block_e[jnp.maximum(n_valid[0] - 1, 0)]
    block_e = jnp.where(block_start < pends[-1], block_e, last_e)
    block_first = jnp.concatenate([jnp.ones((1,), jnp.int32),
                                   (block_e[1:] != block_e[:-1]).astype(jnp.int32)])
    return dest, slot_tok, block_e, block_first, n_valid


def kernel(x_prompt, x_sample, state_s5_re, state_s5_im, state_hgrn, norm_mix, w_in, b_gate, s5_lam_re, s5_lam_im, s5_log_step, s5_b_re, s5_b_im, s5_c_re, s5_c_im, s5_d, s5_w_glu, hgrn_lb_logits, hgrn_norm, w_branch_s5, w_branch_hgrn, w_out, norm_ffn, w_router, b_router, w_gate_up, b_gate_up, w_down, b_down, norm_final):
    depth = w_in.shape[0]
    assert depth == 1, "single-layer stack"
    n_p, s_p, d = x_prompt.shape
    n_s, s_s, _ = x_sample.shape
    t_p, t_s = n_p * s_p, n_s * s_s
    g_s5, p_s5 = s5_lam_re.shape[1:]
    d_s5 = g_s5 * S5_GROUP
    n_heads = state_hgrn.shape[2]
    d_h = n_heads * HGRN_HEAD_DIM
    n_e = w_router.shape[2]
    l = 0

    lb_all = jnp.cumsum(jax.nn.softmax(hgrn_lb_logits.astype(F32), axis=0), axis=0)[:depth]
    x = jnp.concatenate([x_prompt.reshape(t_p, d), x_sample.reshape(t_s, d)], axis=0)
    t = t_p + t_s

    h = _rmsnorm(x, norm_mix[l], BF16)
    z = _dense_mm([(h, 0)], [w_in[l]], [], lambda accs, ex: accs[0], F32, tm=1024, tn=512, name="in_proj")

    wb, wc, lam = _s5_params(s5_lam_re[l], s5_lam_im[l], s5_log_step[l], s5_b_re[l], s5_b_im[l],
                             s5_c_re[l], s5_c_im[l])
    d_flat = s5_d[l].reshape(1, d_s5)
    zeros_s5 = jnp.zeros((n_p, g_s5 * p_s5), F32)
    y_p, p_re, p_im = _s5_scan(z, 0, n_p, s_p, zeros_s5, zeros_s5, wb, wc, lam, d_flat, rb_rows=256)
    y_s, s_re, s_im = _s5_scan(z, t_p, n_s, s_s, state_s5_re[l].reshape(n_s, -1),
                               state_s5_im[l].reshape(n_s, -1), wb, wc, lam, d_flat, rb_rows=256)
    y = jnp.concatenate([y_p, y_s], axis=0)
    a = _dense_mm([(y, 0)], [s5_w_glu[l]],
                  [(y, (1024, 512), lambda n, m: (m, n))],
                  lambda accs, ex: ex[0] * _sigmoid(accs[0]), BF16, tm=1024, tn=512, name="s5_glu")

    lb_flat = lb_all[l].reshape(1, d_h)
    cb0 = d_s5 // d_h
    zeros_hg = jnp.zeros((n_p,) + state_hgrn.shape[2:], F32)
    o_p, p_hg = _hgrn(z, 0, n_p, s_p, zeros_hg, lb_flat, hgrn_norm[l], cb0)
    o_s, s_hg = _hgrn(z, t_p, n_s, s_s, state_hgrn[l], lb_flat, hgrn_norm[l], cb0)
    b = jnp.concatenate([o_p, o_s], axis=0)

    tn = 512
    ga_cb = (d_s5 + 4 * d_h) // tn
    gb_cb = ga_cb + d // tn
    bg = b_gate[l].reshape(1, 2 * d)
    merged = _dense_mm(
        [(a, 0), (b, 0)], [w_branch_s5[l], w_branch_hgrn[l]],
        [(z, (512, tn), lambda n, m: (m, ga_cb + n)),
         (z, (512, tn), lambda n, m: (m, gb_cb + n)),
         (bg, (1, tn), lambda n, m: (0, n)),
         (bg, (1, tn), lambda n, m: (0, d // tn + n))],
        lambda accs, ex: (_sigmoid(ex[0] + ex[2]) * accs[0] + _sigmoid(ex[1] + ex[3]) * accs[1]),
        BF16, tm=512, tn=tn, name="merge")
    x1 = _dense_mm([(merged, 0)], [w_out[l]], [(x, (1024, 512), lambda n, m: (m, n))],
                   lambda accs, ex: ex[0] + accs[0], F32, tm=1024, tn=512, name="out_proj")

    tm_e = 256
    xn, idx_wide, gate_wide = _router(x1, norm_ffn[l], w_router[l], b_router[l])
    top_idx = idx_wide[:, :TOP_K]
    dest, slot_tok, block_e, block_first, n_valid = _routing_tables(top_idx, n_e, tm_e)
    xs = _gather_rows(xn, slot_tok, tm_e)
    act = _expert_mm(xs, w_gate_up[l], b_gate_up[l], block_e, block_first, n_valid, tm_e, 512, swiglu=True)
    yb = _expert_mm(act, w_down[l], b_down[l], block_e, block_first, n_valid, tm_e, 512, swiglu=False)
    y_out = _combine(yb, dest, x1, gate_wide, norm_final, tm=128)

    y_prompt = y_out[:t_p].reshape(n_p, s_p, d)
    y_sample = y_out[t_p:].reshape(n_s, s_s, d)
    return (y_prompt, y_sample,
            p_re.reshape(1, n_p, g_s5, p_s5), p_im.reshape(1, n_p, g_s5, p_s5), p_hg[None],
            s_re.reshape(1, n_s, g_s5, p_s5), s_im.reshape(1, n_s, g_s5, p_s5), s_hg[None])
```

```python
import functools

import jax
import jax.numpy as jnp
from jax import lax
from jax.experimental import pallas as pl
from jax.experimental.pallas import tpu as pltpu

F32 = jnp.float32
BF16 = jnp.bfloat16

RMS_EPS = 1e-5
S5_GROUP = 16
S5_STATE = 64
S5_GROUPS_PER_BLOCK = 16
HGRN_HEAD_DIM = 128
HGRN_CHUNK = 32
TOP_K = 4
SWIGLU_LIMIT = 7.0
SWIGLU_ALPHA = 1.702
EXP_CLAMP = 80.0

V7X_LANES = 128
V7X_SUBLANES = 8
V7X_BF16_SUBLANES = 16
V7X_VMEM_LIMIT = 56 * 1024 * 1024
CAST_ROWS = 256


def _cparams(semantics, vmem=V7X_VMEM_LIMIT):
    return pltpu.CompilerParams(dimension_semantics=semantics, vmem_limit_bytes=vmem)


def _sigmoid(x):
    return 1.0 / (1.0 + jnp.exp(-x))


def _cast_weight(w_ref, wbf_ref):
    k = w_ref.shape[0]
    rows = min(CAST_ROWS, k)

    def body(i, carry):
        r = pl.multiple_of(i * rows, rows)
        wbf_ref[pl.ds(r, rows), :] = w_ref[pl.ds(r, rows), :].astype(BF16)
        return carry

    lax.fori_loop(0, k // rows, body, 0)


def _rmsnorm_body(x_ref, g_ref, o_ref):
    x = x_ref[...]
    ms = jnp.mean(x * x, axis=-1, keepdims=True)
    o_ref[...] = (x * lax.rsqrt(ms + RMS_EPS) * g_ref[...]).astype(o_ref.dtype)


def _rmsnorm(x, g, out_dtype, tm=256):
    t, d = x.shape
    return pl.pallas_call(
        _rmsnorm_body,
        grid=(t // tm,),
        in_specs=[pl.BlockSpec((tm, d), lambda i: (i, 0)),
                  pl.BlockSpec((1, d), lambda i: (0, 0))],
        out_specs=pl.BlockSpec((tm, d), lambda i: (i, 0)),
        out_shape=jax.ShapeDtypeStruct((t, d), out_dtype),
        compiler_params=_cparams(("arbitrary",)),
        name="rmsnorm",
    )(x, g.reshape(1, d))


def _mm_body(*refs, n_x, n_extra, epilogue):
    x_refs = refs[:n_x]
    w_refs = refs[n_x:2 * n_x]
    extra_refs = refs[2 * n_x:2 * n_x + n_extra]
    o_ref = refs[2 * n_x + n_extra]
    wbf_refs = refs[2 * n_x + n_extra + 1:]

    @pl.when(pl.program_id(1) == 0)
    def _():
        for w_ref, wbf_ref in zip(w_refs, wbf_refs):
            _cast_weight(w_ref, wbf_ref)

    accs = [jnp.dot(x_ref[...].astype(BF16), wbf_ref[...], preferred_element_type=F32)
            for x_ref, wbf_ref in zip(x_refs, wbf_refs)]
    o_ref[...] = epilogue(accs, [e[...] for e in extra_refs]).astype(o_ref.dtype)


def _dense_mm(xs, ws, extras, epilogue, out_dtype, tm, tn, name):
    m_total = xs[0][0].shape[0]
    n_total = ws[0].shape[1]
    in_specs = []
    for (x, cb), w in zip(xs, ws):
        in_specs.append(pl.BlockSpec((tm, w.shape[0]), functools.partial(lambda n, m, cb: (m, cb), cb=cb)))
    for w in ws:
        in_specs.append(pl.BlockSpec((w.shape[0], tn), lambda n, m: (0, n)))
    for _, bshape, imap in extras:
        in_specs.append(pl.BlockSpec(bshape, imap))
    body = functools.partial(_mm_body, n_x=len(xs), n_extra=len(extras), epilogue=epilogue)
    return pl.pallas_call(
        body,
        grid=(n_total // tn, m_total // tm),
        in_specs=in_specs,
        out_specs=pl.BlockSpec((tm, tn), lambda n, m: (m, n)),
        out_shape=jax.ShapeDtypeStruct((m_total, n_total), out_dtype),
        scratch_shapes=[pltpu.VMEM((w.shape[0], tn), BF16) for w in ws],
        compiler_params=_cparams(("arbitrary", "arbitrary")),
        name=name,
    )(*[x for x, _ in xs], *ws, *[e for e, _, _ in extras])


def _s5_params(lam_re, lam_im, log_step, b_re, b_im, c_re, c_im):
    g, p = lam_re.shape
    h = b_re.shape[-1]
    gb = S5_GROUPS_PER_BLOCK
    nb = g // gb
    dt = jnp.exp(log_step)[:, None]
    mag = jnp.exp(lam_re * dt)
    ang = lam_im * dt
    lbar_re, lbar_im = mag * jnp.cos(ang), mag * jnp.sin(ang)
    nr, ni = lbar_re - 1.0, lbar_im
    den = lam_re * lam_re + lam_im * lam_im
    fr = (nr * lam_re + ni * lam_im) / den
    fi = (ni * lam_re - nr * lam_im) / den
    bb_re = fr[:, :, None] * b_re - fi[:, :, None] * b_im
    bb_im = fr[:, :, None] * b_im + fi[:, :, None] * b_re
    eye = jnp.eye(gb, dtype=F32)

    def blockdiag_in(bb):
        t = bb.reshape(nb, gb, p, h)
        return jnp.einsum('jgph,gk->jghkp', t, eye).reshape(nb, gb * h, gb * p)

    def blockdiag_out(c):
        t = c.reshape(nb, gb, h, p)
        return jnp.einsum('jghp,gk->jgpkh', t, eye).reshape(nb, gb * p, gb * h)

    wb = jnp.concatenate([blockdiag_in(bb_re), blockdiag_in(bb_im)], axis=2).astype(BF16)
    wc = jnp.concatenate([blockdiag_out(c_re), -blockdiag_out(c_im)], axis=1).astype(BF16)
    lam = jnp.concatenate([lbar_re.reshape(nb, 1, gb * p), lbar_im.reshape(nb, 1, gb * p)], axis=2)
    return wb, wc, lam


def _s5_body(*refs, n_parts, nsb, seq_rows, spb):
    u_refs = refs[:n_parts]
    (sre_ref, sim_ref, wb_ref, wc_ref, lam_ref, d_ref,
     y_ref, nre_ref, nim_ref, bu_ref, xs_ref, cre_ref, cim_ref) = refs[n_parts:]
    hs = cre_ref.shape[1]
    pr = u_refs[0].shape[0]
    nc = hs // V7X_LANES
    for p, u_ref in enumerate(u_refs):
        bu = jnp.dot(u_ref[...].astype(BF16), wb_ref[...], preferred_element_type=F32)
        for c in range(2 * nc):
            bu_ref[c, p * pr:(p + 1) * pr, :] = bu[:, c * V7X_LANES:(c + 1) * V7X_LANES]

    @pl.when(pl.program_id(1) % spb == 0)
    def _():
        cre_ref[...] = sre_ref[...]
        cim_ref[...] = sim_ref[...]

    def lanes(c):
        return slice(c * V7X_LANES, (c + 1) * V7X_LANES)

    for g0 in range(0, nsb, V7X_SUBLANES):
        ns = min(V7X_SUBLANES, nsb - g0)
        lr = [jnp.broadcast_to(lam_ref[:, lanes(c)], (ns, V7X_LANES)) for c in range(nc)]
        li = [jnp.broadcast_to(lam_ref[:, lanes(nc + c)], (ns, V7X_LANES)) for c in range(nc)]

        def t_body(t, carry, g0=g0, ns=ns, lr=lr, li=li):
            xr, xi = carry
            rows = pl.ds(g0 * seq_rows + t, ns, stride=seq_rows)
            nr, ni = [], []
            for c in range(nc):
                r = lr[c] * xr[c] - li[c] * xi[c] + bu_ref[c, rows, :]
                i = lr[c] * xi[c] + li[c] * xr[c] + bu_ref[nc + c, rows, :]
                xs_ref[c, rows, :] = r
                xs_ref[nc + c, rows, :] = i
                nr.append(r)
                ni.append(i)
            return tuple(nr), tuple(ni)

        init = (tuple(cre_ref[g0:g0 + ns, lanes(c)] for c in range(nc)),
                tuple(cim_ref[g0:g0 + ns, lanes(c)] for c in range(nc)))
        xr, xi = lax.fori_loop(0, seq_rows, t_body, init, unroll=8)
        for c in range(nc):
            cre_ref[g0:g0 + ns, lanes(c)] = xr[c]
            cim_ref[g0:g0 + ns, lanes(c)] = xi[c]

    for p, u_ref in enumerate(u_refs):
        xs = jnp.concatenate([xs_ref[c, p * pr:(p + 1) * pr, :].astype(BF16) for c in range(2 * nc)],
                             axis=1)
        y = jnp.dot(xs, wc_ref[...], preferred_element_type=F32) + d_ref[...] * u_ref[...]
        y_ref[p] = jax.nn.gelu(y)
    nre_ref[...] = cre_ref[...]
    nim_ref[...] = cim_ref[...]


def _s5_scan(z, row0, n_seq, seq_len, st_re, st_im, wb, wc, lam, d_flat, part_rows):
    nb, bw, sw = wb.shape
    hs = sw // 2
    if seq_len >= part_rows:
        n_parts, nsb, seq_rows = n_seq, n_seq, part_rows
        n_rb = seq_len // part_rows
        spb = n_rb
    else:
        n_parts, nsb, seq_rows = 1, part_rows // seq_len, seq_len
        n_rb = n_seq * seq_len // part_rows
        spb = 1
    n_sblk = n_rb // spb
    rb0 = row0 // part_rows
    part_stride = seq_len // part_rows if n_parts > 1 else 0
    st_re3 = st_re.reshape(n_sblk, nsb, nb * hs)
    st_im3 = st_im.reshape(n_sblk, nsb, nb * hs)
    body = functools.partial(_s5_body, n_parts=n_parts, nsb=nsb, seq_rows=seq_rows, spb=spb)
    state_spec = pl.BlockSpec((None, nsb, hs), lambda j, r: (r // spb, 0, j))
    u_specs = [pl.BlockSpec((part_rows, bw),
                            functools.partial(lambda j, r, p: (rb0 + p * part_stride + r, j), p=p))
               for p in range(n_parts)]
    rows_per_part = n_seq * seq_len // n_parts
    y, nre, nim = pl.pallas_call(
        body,
        grid=(nb, n_rb),
        in_specs=u_specs + [
            state_spec, state_spec,
            pl.BlockSpec((None, bw, sw), lambda j, r: (j, 0, 0)),
            pl.BlockSpec((None, sw, bw), lambda j, r: (j, 0, 0)),
            pl.BlockSpec((None, 1, sw), lambda j, r: (j, 0, 0)),
            pl.BlockSpec((1, bw), lambda j, r: (0, j))],
        out_specs=[pl.BlockSpec((n_parts, part_rows, bw), lambda j, r: (0, r, j)),
                   state_spec, state_spec],
        out_shape=[jax.ShapeDtypeStruct((n_parts, rows_per_part, nb * bw), F32),
                   jax.ShapeDtypeStruct((n_sblk, nsb, nb * hs), F32),
                   jax.ShapeDtypeStruct((n_sblk, nsb, nb * hs), F32)],
        scratch_shapes=[pltpu.VMEM((sw // V7X_LANES, n_parts * part_rows, V7X_LANES), F32),
                        pltpu.VMEM((sw // V7X_LANES, n_parts * part_rows, V7X_LANES), F32),
                        pltpu.VMEM((nsb, hs), F32),
                        pltpu.VMEM((nsb, hs), F32)],
        compiler_params=_cparams(("arbitrary", "arbitrary")),
        name="s5_scan",
    )(*([z] * n_parts), st_re3, st_im3, wb, wc, lam, d_flat)
    return (y.reshape(n_seq * seq_len, nb * bw),
            nre.reshape(n_seq, nb * hs), nim.reshape(n_seq, nb * hs))


def _cumsum_rows(x, seg):
    row = lax.broadcasted_iota(jnp.int32, x.shape, 0) % seg
    s = 1
    while s < seg:
        x = x + jnp.where(row >= s, pltpu.roll(x, s, axis=0), 0.0)
        s *= 2
    return x


def _hgrn_body(q_ref, f_ref, v_ref, g_ref, st_ref, lb_ref, ng_ref, o_ref, ns_ref, s_ref,
               *, rows, prows, n_chunks, n_heads):
    kd = HGRN_HEAD_DIM
    c = pl.program_id(1)

    @pl.when(c == 0)
    def _():
        s_ref[...] = st_ref[...]

    lb = lb_ref[...]
    fz = f_ref[...]
    q = q_ref[...]
    log_f = jnp.log(lb + (1.0 - lb) * _sigmoid(fz))
    kk = (1.0 - lb) * _sigmoid(-fz)
    qh = q * _sigmoid(q) * (kd ** -0.5)
    v = v_ref[...]
    if prows > rows:
        pad = jnp.zeros((prows - rows, fz.shape[1]), F32)
        log_f, kk, qh, v = [jnp.concatenate([a, pad], axis=0) for a in (log_f, kk, qh, v)]
    b = _cumsum_rows(log_f, prows)
    b_end = b[rows - 1:rows, :]
    b_mid = b[rows // 2 - 1:rows // 2, :]
    qt = (qh * jnp.exp(jnp.minimum(b - b_mid, EXP_CLAMP))).astype(BF16)
    kt = (kk * jnp.exp(jnp.minimum(b_mid - b, EXP_CLAMP))).astype(BF16)
    qe = (qh * jnp.exp(b)).astype(BF16)
    ke = (kk * jnp.exp(b_end - b)).astype(BF16)
    dec = jnp.exp(b_end)
    vb = v.astype(BF16)
    causal = (lax.broadcasted_iota(jnp.int32, (prows, prows), 0)
              >= lax.broadcasted_iota(jnp.int32, (prows, prows), 1))
    eye = (lax.broadcasted_iota(jnp.int32, (kd, kd), 0)
           == lax.broadcasted_iota(jnp.int32, (kd, kd), 1))
    og = g_ref[...]
    gate = og * _sigmoid(og)
    ng = ng_ref[...]
    for h in range(n_heads):
        sl = slice(h * kd, (h + 1) * kd)
        sc = lax.dot_general(qt[:, sl], kt[:, sl], (((1,), (1,)), ((), ())),
                             preferred_element_type=F32)
        sc = jnp.where(causal, sc, 0.0).astype(BF16)
        s_prev = s_ref[h]
        o = (jnp.dot(sc, vb[:, sl], preferred_element_type=F32)
             + jnp.dot(qe[:, sl], s_prev.astype(BF16), preferred_element_type=F32))
        upd = lax.dot_general(ke[:, sl], vb[:, sl], (((0,), (0,)), ((), ())),
                              preferred_element_type=F32)
        dec_col = jnp.sum(jnp.where(eye, jnp.broadcast_to(dec[:, sl], (kd, kd)), 0.0),
                          axis=1, keepdims=True)
        s_ref[h] = dec_col * s_prev + upd
        o = o[:rows]
        o = o * lax.rsqrt(jnp.mean(o * o, axis=-1, keepdims=True) + RMS_EPS) * ng
        o_ref[:, sl] = o * gate[:, sl]

    @pl.when(c == n_chunks - 1)
    def _():
        ns_ref[...] = s_ref[...]


def _hgrn(z, row0, n_seq, seq_len, state, lb_flat, norm_g, col_block0):
    n_heads, kd = state.shape[1], state.shape[2]
    dh = n_heads * kd
    rows = min(HGRN_CHUNK, seq_len)
    prows = max(rows, V7X_BF16_SUBLANES)
    n_chunks = seq_len // rows
    rb0 = row0 // rows
    body = functools.partial(_hgrn_body, rows=rows, prows=prows, n_chunks=n_chunks, n_heads=n_heads)

    def zspec(i):
        return pl.BlockSpec((rows, dh), lambda s, c: (rb0 + s * n_chunks + c, col_block0 + i))

    st_spec = pl.BlockSpec((None, n_heads, kd, kd), lambda s, c: (s, 0, 0, 0))
    o, ns = pl.pallas_call(
        body,
        grid=(n_seq, n_chunks),
        in_specs=[zspec(0), zspec(1), zspec(2), zspec(3), st_spec,
                  pl.BlockSpec((1, dh), lambda s, c: (0, 0)),
                  pl.BlockSpec((1, kd), lambda s, c: (0, 0))],
        out_specs=[pl.BlockSpec((rows, dh), lambda s, c: (s * n_chunks + c, 0)), st_spec],
        out_shape=[jax.ShapeDtypeStruct((n_seq * seq_len, dh), F32),
                   jax.ShapeDtypeStruct(state.shape, F32)],
        scratch_shapes=[pltpu.VMEM((n_heads, kd, kd), F32)],
        compiler_params=_cparams(("arbitrary", "arbitrary")),
        name="hgrn",
    )(z, z, z, z, state, lb_flat, norm_g.reshape(1, kd))
    return o, ns


def _router_body(x_ref, g_ref, w_ref, b_ref, xn_ref, idx_ref, gate_ref):
    x = x_ref[...]
    ms = jnp.mean(x * x, axis=-1, keepdims=True)
    xn = x * lax.rsqrt(ms + RMS_EPS) * g_ref[...]
    half = xn.shape[1] // 2
    lo = lax.bitcast_convert_type(xn[:, :half].astype(BF16).astype(F32), jnp.uint32)
    hi = lax.bitcast_convert_type(xn[:, half:].astype(BF16).astype(F32), jnp.uint32)
    xn_ref[...] = (lo >> 16) | (hi & jnp.uint32(0xFFFF0000))
    logits = jnp.dot(xn.astype(BF16), w_ref[...].astype(BF16), preferred_element_type=F32) + b_ref[...]
    n_e = logits.shape[1]
    lane = lax.broadcasted_iota(jnp.int32, logits.shape, 1)
    out_lane = lax.broadcasted_iota(jnp.int32, idx_ref.shape, 1)
    idx_out = jnp.zeros(idx_ref.shape, jnp.int32)
    val_out = jnp.full(gate_ref.shape, -jnp.inf, F32)
    work = logits
    for k in range(TOP_K):
        m = jnp.max(work, axis=-1, keepdims=True)
        i = jnp.min(jnp.where(work == m, lane, n_e), axis=-1, keepdims=True)
        idx_out = jnp.where(out_lane == k, i, idx_out)
        val_out = jnp.where(out_lane == k, m, val_out)
        work = jnp.where(lane == i, -jnp.inf, work)
    top = jnp.max(val_out, axis=-1, keepdims=True)
    e = jnp.exp(val_out - top)
    gate_ref[...] = e / jnp.sum(e, axis=-1, keepdims=True)
    idx_ref[...] = idx_out


def _router(x, g, w_router, b_router, tm=256):
    t, d = x.shape
    n_e = w_router.shape[1]
    row = lambda i: (i, 0)
    fixed = lambda i: (0, 0)
    return pl.pallas_call(
        _router_body,
        grid=(t // tm,),
        in_specs=[pl.BlockSpec((tm, d), row), pl.BlockSpec((1, d), fixed),
                  pl.BlockSpec((d, n_e), fixed), pl.BlockSpec((1, n_e), fixed)],
        out_specs=[pl.BlockSpec((tm, d // 2), row), pl.BlockSpec((tm, V7X_LANES), row),
                   pl.BlockSpec((tm, V7X_LANES), row)],
        out_shape=[jax.ShapeDtypeStruct((t, d // 2), jnp.uint32),
                   jax.ShapeDtypeStruct((t, V7X_LANES), jnp.int32),
                   jax.ShapeDtypeStruct((t, V7X_LANES), F32)],
        compiler_params=_cparams(("arbitrary",)),
        name="router",
    )(x, g.reshape(1, d), w_router, b_router.reshape(1, n_e))


def _gather_rows_copy(src_hbm, buf_ref, sem_ref, tok, slot, r):
    return pltpu.make_async_copy(src_hbm.at[pl.ds(tok, 1), :],
                                 buf_ref.at[slot, pl.ds(r, 1), :],
                                 sem_ref.at[slot])


def _gather_body(tok_ref, x_hbm, o_ref, buf_ref, sem_ref, *, tm):
    i = pl.program_id(0)
    n = pl.num_programs(0)

    def issue(block, slot):
        def body(r, carry):
            _gather_rows_copy(x_hbm, buf_ref, sem_ref, tok_ref[block * tm + r], slot, r).start()
            return carry
        lax.fori_loop(0, tm, body, 0, unroll=8)

    @pl.when(i == 0)
    def _():
        issue(0, 0)

    @pl.when(i + 1 < n)
    def _():
        issue(i + 1, (i + 1) % 2)

    slot = i % 2
    pltpu.make_async_copy(x_hbm.at[pl.ds(0, tm), :], buf_ref.at[slot], sem_ref.at[slot]).wait()
    w = buf_ref[slot]
    half = w.shape[1]
    o_ref[:, :half] = lax.bitcast_convert_type(w << 16, F32).astype(BF16)
    o_ref[:, half:] = lax.bitcast_convert_type(w & jnp.uint32(0xFFFF0000), F32).astype(BF16)


def _gather_rows(x, tok, tm):
    r_total = tok.shape[0]
    d = 2 * x.shape[1]
    return pl.pallas_call(
        functools.partial(_gather_body, tm=tm),
        grid_spec=pltpu.PrefetchScalarGridSpec(
            num_scalar_prefetch=1,
            grid=(r_total // tm,),
            in_specs=[pl.BlockSpec(memory_space=pl.ANY)],
            out_specs=pl.BlockSpec((tm, d), lambda i, tok: (i, 0)),
            scratch_shapes=[pltpu.VMEM((2, tm, d // 2), x.dtype), pltpu.SemaphoreType.DMA((2,))]),
        out_shape=jax.ShapeDtypeStruct((r_total, d), BF16),
        compiler_params=_cparams(("arbitrary",)),
        name="moe_gather",
    )(tok, x)


def _swiglu_chunk(h, sel):
    gate = jnp.minimum(h, SWIGLU_LIMIT)
    up = jnp.clip(h, -SWIGLU_LIMIT, SWIGLU_LIMIT)
    glu = gate * _sigmoid(SWIGLU_ALPHA * gate)
    halves = []
    for c in range(2):
        sl = slice(c * V7X_LANES, (c + 1) * V7X_LANES)
        up_next = pltpu.roll(up[:, sl], V7X_LANES - 1, axis=1)
        halves.append(((up_next + 1.0) * glu[:, sl]).astype(BF16))
    prod = jnp.concatenate(halves, axis=1)
    return jnp.dot(prod, sel, preferred_element_type=F32).astype(BF16)


def _expert_mm_body(bstart_ref, nblk_ref, x_hbm, w_ref, b_ref, *rest, tm, swiglu, n_blocks):
    if swiglu:
        sel_ref, o_hbm, wbf_ref, xbuf, obuf, xsem, osem = rest
    else:
        o_hbm, wbf_ref, xbuf, obuf, xsem, osem = rest
    n = pl.program_id(0)
    e = pl.program_id(1)
    nb = nblk_ref[e]
    b0 = bstart_ref[e]
    otn = obuf.shape[2]
    col0 = pl.multiple_of(n * otn, otn)
    chunk = 2 * V7X_LANES

    def x_copy(blk, slot):
        r = pl.multiple_of(blk * tm, tm)
        return pltpu.make_async_copy(x_hbm.at[pl.ds(r, tm), :], xbuf.at[slot], xsem.at[slot])

    def o_copy(blk, slot):
        r = pl.multiple_of(blk * tm, tm)
        return pltpu.make_async_copy(obuf.at[slot], o_hbm.at[pl.ds(r, tm), pl.ds(col0, otn)],
                                     osem.at[slot])

    @pl.when(nb > 0)
    def _():
        x_copy(b0, 0).start()
        _cast_weight(w_ref, wbf_ref)

    def body(i, carry):
        slot = i % 2
        x_copy(b0 + i, slot).wait()

        @pl.when(i + 1 < nb)
        def _():
            x_copy(b0 + i + 1, 1 - slot).start()

        @pl.when(i >= 2)
        def _():
            o_copy(b0 + i - 2, slot).wait()

        x = xbuf[slot]
        n_chunks = wbf_ref.shape[1] // chunk

        def column_chunk(c):
            cs = slice(c * chunk, (c + 1) * chunk)
            return jnp.dot(x, wbf_ref[:, cs], preferred_element_type=F32) + b_ref[:, cs]

        h = column_chunk(0)
        for c in range(n_chunks):
            h_next = column_chunk(c + 1) if c + 1 < n_chunks else None
            if swiglu:
                obuf[slot, :, c * V7X_LANES:(c + 1) * V7X_LANES] = _swiglu_chunk(h, sel_ref[...])
            else:
                obuf[slot, :, c * chunk:(c + 1) * chunk] = h
            h = h_next
        o_copy(b0 + i, slot).start()
        return carry

    lax.fori_loop(0, nb, body, 0)

    @pl.when(nb >= 2)
    def _():
        o_copy(b0 + nb - 2, nb % 2).wait()

    @pl.when(nb >= 1)
    def _():
        o_copy(b0 + nb - 1, (nb - 1) % 2).wait()

    @pl.when(e == pl.num_programs(1) - 1)
    def _():
        obuf[0] = jnp.zeros(obuf.shape[1:], obuf.dtype)

        def zstart(blk, carry):
            o_copy(blk, 0).start()
            return carry

        def zwait(blk, carry):
            o_copy(blk, 0).wait()
            return carry

        lax.fori_loop(b0 + nb, n_blocks, zstart, 0)
        lax.fori_loop(b0 + nb, n_blocks, zwait, 0)


def _expert_mm(x, w, bias, bstart, nblk, tm, tn, swiglu):
    r_total, k = x.shape
    n_e, _, n_total = w.shape
    out_tn = tn // 2 if swiglu else tn
    out_n = n_total // 2 if swiglu else n_total
    out_dtype = BF16 if swiglu else F32
    in_specs = [pl.BlockSpec(memory_space=pl.ANY),
                pl.BlockSpec((None, k, tn), lambda n, e, bs, nb: (e, 0, n)),
                pl.BlockSpec((None, 1, tn), lambda n, e, bs, nb: (e, 0, n))]
    args = [x, w, bias.reshape(n_e, 1, n_total)]
    if swiglu:
        sel = (lax.broadcasted_iota(jnp.int32, (2 * V7X_LANES, V7X_LANES), 0)
               == 2 * lax.broadcasted_iota(jnp.int32, (2 * V7X_LANES, V7X_LANES), 1)).astype(BF16)
        in_specs.append(pl.BlockSpec((2 * V7X_LANES, V7X_LANES), lambda n, e, bs, nb: (0, 0)))
        args.append(sel)
    return pl.pallas_call(
        functools.partial(_expert_mm_body, tm=tm, swiglu=swiglu, n_blocks=r_total // tm),
        grid_spec=pltpu.PrefetchScalarGridSpec(
            num_scalar_prefetch=2,
            grid=(n_total // tn, n_e),
            in_specs=in_specs,
            out_specs=pl.BlockSpec(memory_space=pl.ANY),
            scratch_shapes=[pltpu.VMEM((k, tn), BF16),
                            pltpu.VMEM((2, tm, k), BF16),
                            pltpu.VMEM((2, tm, out_tn), out_dtype),
                            pltpu.SemaphoreType.DMA((2,)),
                            pltpu.SemaphoreType.DMA((2,))]),
        out_shape=jax.ShapeDtypeStruct((r_total, out_n), out_dtype),
        compiler_params=_cparams(("arbitrary", "arbitrary")),
        name="moe_gate_up" if swiglu else "moe_down",
    )(bstart, nblk, *args)


def _combine_rows_copy(src_hbm, buf_ref, sem_ref, row, slot, k, r):
    return pltpu.make_async_copy(src_hbm.at[pl.ds(row, 1), :],
                                 buf_ref.at[slot, k, pl.ds(r, 1), :],
                                 sem_ref.at[slot])


def _combine_body(dest_ref, yb_hbm, x_ref, gate_ref, g_ref, o_ref, buf_ref, sem_ref, *, tm):
    i = pl.program_id(0)
    n = pl.num_programs(0)

    def issue(block, slot):
        def body(r, carry):
            for k in range(TOP_K):
                row = dest_ref[(block * tm + r) * TOP_K + k]
                _combine_rows_copy(yb_hbm, buf_ref, sem_ref, row, slot, k, r).start()
            return carry
        lax.fori_loop(0, tm, body, 0, unroll=2)

    @pl.when(i == 0)
    def _():
        issue(0, 0)

    @pl.when(i + 1 < n)
    def _():
        issue(i + 1, (i + 1) % 2)

    slot = i % 2
    for k in range(TOP_K):
        pltpu.make_async_copy(yb_hbm.at[pl.ds(0, tm), :], buf_ref.at[slot, k], sem_ref.at[slot]).wait()
    gates = gate_ref[...]
    acc = x_ref[...]
    for k in range(TOP_K):
        acc = acc + gates[:, k:k + 1] * buf_ref[slot, k]
    ms = jnp.mean(acc * acc, axis=-1, keepdims=True)
    o_ref[...] = acc * lax.rsqrt(ms + RMS_EPS) * g_ref[...]


def _combine(yb, dest_flat, x, gates, g, tm):
    t, d = x.shape
    return pl.pallas_call(
        functools.partial(_combine_body, tm=tm),
        grid_spec=pltpu.PrefetchScalarGridSpec(
            num_scalar_prefetch=1,
            grid=(t // tm,),
            in_specs=[pl.BlockSpec(memory_space=pl.ANY),
                      pl.BlockSpec((tm, d), lambda i, dest: (i, 0)),
                      pl.BlockSpec((tm, V7X_LANES), lambda i, dest: (i, 0)),
                      pl.BlockSpec((1, d), lambda i, dest: (0, 0))],
            out_specs=pl.BlockSpec((tm, d), lambda i, dest: (i, 0)),
            scratch_shapes=[pltpu.VMEM((2, TOP_K, tm, d), F32), pltpu.SemaphoreType.DMA((2,))]),
        out_shape=jax.ShapeDtypeStruct((t, d), F32),
        compiler_params=_cparams(("arbitrary",)),
        name="moe_combine",
    )(dest_flat, yb, x, gates, g.reshape(1, d))


def _routing_tables(top_idx, n_e, tm):
    t = top_idx.shape[0]
    p = t * TOP_K
    flat_e = top_idx.reshape(p)
    onehot = (flat_e[:, None] == jnp.arange(n_e, dtype=jnp.int32)[None, :]).astype(jnp.int32)
    rank = jnp.take_along_axis(jnp.cumsum(onehot, axis=0), flat_e[:, None], axis=1)[:, 0] - 1
    counts = jnp.sum(onehot, axis=0)
    pcounts = (counts + tm - 1) // tm * tm
    pends = jnp.cumsum(pcounts)
    pstarts = pends - pcounts
    dest = (pstarts[flat_e] + rank).astype(jnp.int32)
    n_blocks = p // tm + n_e
    flat_tok = jnp.arange(p, dtype=jnp.int32) // TOP_K
    slot_tok = jnp.zeros((n_blocks * tm,), jnp.int32).at[dest].set(flat_tok)
    return dest, slot_tok, (pstarts // tm).astype(jnp.int32), (pcounts // tm).astype(jnp.int32)


def kernel(x_prompt, x_sample, state_s5_re, state_s5_im, state_hgrn, norm_mix, w_in, b_gate, s5_lam_re, s5_lam_im, s5_log_step, s5_b_re, s5_b_im, s5_c_re, s5_c_im, s5_d, s5_w_glu, hgrn_lb_logits, hgrn_norm, w_branch_s5, w_branch_hgrn, w_out, norm_ffn, w_router, b_router, w_gate_up, b_gate_up, w_down, b_down, norm_final):
    depth = w_in.shape[0]
    assert depth == 1, "single-layer stack"
    n_p, s_p, d = x_prompt.shape
    n_s, s_s, _ = x_sample.shape
    t_p, t_s = n_p * s_p, n_s * s_s
    g_s5, p_s5 = s5_lam_re.shape[1:]
    d_s5 = g_s5 * S5_GROUP
    n_heads = state_hgrn.shape[2]
    d_h = n_heads * HGRN_HEAD_DIM
    n_e = w_router.shape[2]
    l = 0

    lb_all = jnp.cumsum(jax.nn.softmax(hgrn_lb_logits.astype(F32), axis=0), axis=0)[:depth]
    x = jnp.concatenate([x_prompt.reshape(t_p, d), x_sample.reshape(t_s, d)], axis=0)
    t = t_p + t_s

    h = _rmsnorm(x, norm_mix[l], BF16)
    z = _dense_mm([(h, 0)], [w_in[l]], [], lambda accs, ex: accs[0], F32, tm=1024, tn=512, name="in_proj")

    wb, wc, lam = _s5_params(s5_lam_re[l], s5_lam_im[l], s5_log_step[l], s5_b_re[l], s5_b_im[l],
                             s5_c_re[l], s5_c_im[l])
    d_flat = s5_d[l].reshape(1, d_s5)
    zeros_s5 = jnp.zeros((n_p, g_s5 * p_s5), F32)
    y_p, p_re, p_im = _s5_scan(z, 0, n_p, s_p, zeros_s5, zeros_s5, wb, wc, lam, d_flat, part_rows=256)
    y_s, s_re, s_im = _s5_scan(z, t_p, n_s, s_s, state_s5_re[l].reshape(n_s, -1),
                               state_s5_im[l].reshape(n_s, -1), wb, wc, lam, d_flat, part_rows=256)
    y = jnp.concatenate([y_p, y_s], axis=0)
    a = _dense_mm([(y, 0)], [s5_w_glu[l]],
                  [(y, (1024, 512), lambda n, m: (m, n))],
                  lambda accs, ex: ex[0] * _sigmoid(accs[0]), BF16, tm=1024, tn=512, name="s5_glu")

    lb_flat = lb_all[l].reshape(1, d_h)
    cb0 = d_s5 // d_h
    zeros_hg = jnp.zeros((n_p,) + state_hgrn.shape[2:], F32)
    o_p, p_hg = _hgrn(z, 0, n_p, s_p, zeros_hg, lb_flat, hgrn_norm[l], cb0)
    o_s, s_hg = _hgrn(z, t_p, n_s, s_s, state_hgrn[l], lb_flat, hgrn_norm[l], cb0)
    b = jnp.concatenate([o_p, o_s], axis=0)

    tn = 512
    ga_cb = (d_s5 + 4 * d_h) // tn
    gb_cb = ga_cb + d // tn
    bg = b_gate[l].reshape(1, 2 * d)
    merged = _dense_mm(
        [(a, 0), (b, 0)], [w_branch_s5[l], w_branch_hgrn[l]],
        [(z, (512, tn), lambda n, m: (m, ga_cb + n)),
         (z, (512, tn), lambda n, m: (m, gb_cb + n)),
         (bg, (1, tn), lambda n, m: (0, n)),
         (bg, (1, tn), lambda n, m: (0, d // tn + n))],
        lambda accs, ex: (_sigmoid(ex[0] + ex[2]) * accs[0] + _sigmoid(ex[1] + ex[3]) * accs[1]),
        BF16, tm=512, tn=tn, name="merge")
    x1 = _dense_mm([(merged, 0)], [w_out[l]], [(x, (1024, 512), lambda n, m: (m, n))],
                   lambda accs, ex: ex[0] + accs[0], F32, tm=1024, tn=512, name="out_proj")

    tm_e = 256
    xn, idx_wide, gate_wide = _router(x1, norm_ffn[l], w_router[l], b_router[l])
    top_idx = idx_wide[:, :TOP_K]
    dest, slot_tok, bstart, nblk = _routing_tables(top_idx, n_e, tm_e)
    xs = _gather_rows(xn, slot_tok, tm_e)
    act = _expert_mm(xs, w_gate_up[l], b_gate_up[l], bstart, nblk, tm_e, 1024, swiglu=True)
    yb = _expert_mm(act, w_down[l], b_down[l], bstart, nblk, tm_e, 1024, swiglu=False)
    y_out = _combine(yb, dest, x1, gate_wide, norm_final, tm=128)

    y_prompt = y_out[:t_p].reshape(n_p, s_p, d)
    y_sample = y_out[t_p:].reshape(n_s, s_s, d)
    return (y_prompt, y_sample,
            p_re.reshape(1, n_p, g_s5, p_s5), p_im.reshape(1, n_p, g_s5, p_s5), p_hg[None],
            s_re.reshape(1, n_s, g_s5, p_s5), s_im.reshape(1, n_s, g_s5, p_s5), s_hg[None])
```

```python
import functools

import jax
import jax.numpy as jnp
from jax import lax
from jax.experimental import pallas as pl
from jax.experimental.pallas import tpu as pltpu

F32 = jnp.float32
BF16 = jnp.bfloat16

RMS_EPS = 1e-5
S5_GROUP = 16
S5_STATE = 64
S5_GROUPS_PER_BLOCK = 16
HGRN_HEAD_DIM = 128
HGRN_CHUNK = 32
TOP_K = 4
SWIGLU_LIMIT = 7.0
SWIGLU_ALPHA = 1.702
EXP_CLAMP = 80.0

V7X_LANES = 128
V7X_SUBLANES = 8
V7X_BF16_SUBLANES = 16
V7X_VMEM_LIMIT = 56 * 1024 * 1024
CAST_ROWS = 256


def _cparams(semantics, vmem=V7X_VMEM_LIMIT):
    return pltpu.CompilerParams(dimension_semantics=semantics, vmem_limit_bytes=vmem)


def _sigmoid(x):
    return 1.0 / (1.0 + jnp.exp(-x))


def _cast_weight(w_ref, wbf_ref):
    k = w_ref.shape[0]
    rows = min(CAST_ROWS, k)

    def body(i, carry):
        r = pl.multiple_of(i * rows, rows)
        wbf_ref[pl.ds(r, rows), :] = w_ref[pl.ds(r, rows), :].astype(BF16)
        return carry

    lax.fori_loop(0, k // rows, body, 0)


def _rmsnorm_body(x_ref, g_ref, o_ref):
    x = x_ref[...]
    ms = jnp.mean(x * x, axis=-1, keepdims=True)
    o_ref[...] = (x * lax.rsqrt(ms + RMS_EPS) * g_ref[...]).astype(o_ref.dtype)


def _rmsnorm(x, g, out_dtype, tm=256):
    t, d = x.shape
    return pl.pallas_call(
        _rmsnorm_body,
        grid=(t // tm,),
        in_specs=[pl.BlockSpec((tm, d), lambda i: (i, 0)),
                  pl.BlockSpec((1, d), lambda i: (0, 0))],
        out_specs=pl.BlockSpec((tm, d), lambda i: (i, 0)),
        out_shape=jax.ShapeDtypeStruct((t, d), out_dtype),
        compiler_params=_cparams(("arbitrary",)),
        name="rmsnorm",
    )(x, g.reshape(1, d))


def _mm_body(*refs, n_x, n_extra, epilogue):
    x_refs = refs[:n_x]
    w_refs = refs[n_x:2 * n_x]
    extra_refs = refs[2 * n_x:2 * n_x + n_extra]
    o_ref = refs[2 * n_x + n_extra]
    wbf_refs = refs[2 * n_x + n_extra + 1:]

    @pl.when(pl.program_id(1) == 0)
    def _():
        for w_ref, wbf_ref in zip(w_refs, wbf_refs):
            _cast_weight(w_ref, wbf_ref)

    accs = [jnp.dot(x_ref[...].astype(BF16), wbf_ref[...], preferred_element_type=F32)
            for x_ref, wbf_ref in zip(x_refs, wbf_refs)]
    o_ref[...] = epilogue(accs, [e[...] for e in extra_refs]).astype(o_ref.dtype)


def _dense_mm(xs, ws, extras, epilogue, out_dtype, tm, tn, name):
    m_total = xs[0][0].shape[0]
    n_total = ws[0].shape[1]
    in_specs = []
    for (x, cb), w in zip(xs, ws):
        in_specs.append(pl.BlockSpec((tm, w.shape[0]), functools.partial(lambda n, m, cb: (m, cb), cb=cb)))
    for w in ws:
        in_specs.append(pl.BlockSpec((w.shape[0], tn), lambda n, m: (0, n)))
    for _, bshape, imap in extras:
        in_specs.append(pl.BlockSpec(bshape, imap))
    body = functools.partial(_mm_body, n_x=len(xs), n_extra=len(extras), epilogue=epilogue)
    return pl.pallas_call(
        body,
        grid=(n_total // tn, m_total // tm),
        in_specs=in_specs,
        out_specs=pl.BlockSpec((tm, tn), lambda n, m: (m, n)),
        out_shape=jax.ShapeDtypeStruct((m_total, n_total), out_dtype),
        scratch_shapes=[pltpu.VMEM((w.shape[0], tn), BF16) for w in ws],
        compiler_params=_cparams(("arbitrary", "arbitrary")),
        name=name,
    )(*[x for x, _ in xs], *ws, *[e for e, _, _ in extras])


def _s5_params(lam_re, lam_im, log_step, b_re, b_im, c_re, c_im):
    g, p = lam_re.shape
    h = b_re.shape[-1]
    gb = S5_GROUPS_PER_BLOCK
    nb = g // gb
    dt = jnp.exp(log_step)[:, None]
    mag = jnp.exp(lam_re * dt)
    ang = lam_im * dt
    lbar_re, lbar_im = mag * jnp.cos(ang), mag * jnp.sin(ang)
    nr, ni = lbar_re - 1.0, lbar_im
    den = lam_re * lam_re + lam_im * lam_im
    fr = (nr * lam_re + ni * lam_im) / den
    fi = (ni * lam_re - nr * lam_im) / den
    bb_re = fr[:, :, None] * b_re - fi[:, :, None] * b_im
    bb_im = fr[:, :, None] * b_im + fi[:, :, None] * b_re
    eye = jnp.eye(gb, dtype=F32)

    def blockdiag_in(bb):
        t = bb.reshape(nb, gb, p, h)
        return jnp.einsum('jgph,gk->jghkp', t, eye).reshape(nb, gb * h, gb * p)

    def blockdiag_out(c):
        t = c.reshape(nb, gb, h, p)
        return jnp.einsum('jghp,gk->jgpkh', t, eye).reshape(nb, gb * p, gb * h)

    wb = jnp.concatenate([blockdiag_in(bb_re), blockdiag_in(bb_im)], axis=2).astype(BF16)
    wc = jnp.concatenate([blockdiag_out(c_re), -blockdiag_out(c_im)], axis=1).astype(BF16)
    lam = jnp.concatenate([lbar_re.reshape(nb, 1, gb * p), lbar_im.reshape(nb, 1, gb * p)], axis=2)
    return wb, wc, lam


def _s5_body(*refs, n_parts, nsb, seq_rows, spb, part_pitch, seq_pitch):
    u_refs = refs[:n_parts]
    (sre_ref, sim_ref, wb_ref, wc_ref, lam_ref, d_ref,
     y_ref, nre_ref, nim_ref, bu_ref, xs_ref, cre_ref, cim_ref) = refs[n_parts:]
    hs = cre_ref.shape[1]
    pr = u_refs[0].shape[0]
    nc = hs // V7X_LANES
    for p, u_ref in enumerate(u_refs):
        bu = jnp.dot(u_ref[...].astype(BF16), wb_ref[...], preferred_element_type=F32)
        for c in range(2 * nc):
            bu_ref[c, p * part_pitch:p * part_pitch + pr, :] = bu[:, c * V7X_LANES:(c + 1) * V7X_LANES]

    @pl.when(pl.program_id(1) % spb == 0)
    def _():
        cre_ref[...] = sre_ref[...]
        cim_ref[...] = sim_ref[...]

    def lanes(c):
        return slice(c * V7X_LANES, (c + 1) * V7X_LANES)

    for g0 in range(0, nsb, V7X_SUBLANES):
        ns = min(V7X_SUBLANES, nsb - g0)
        lr = [jnp.broadcast_to(lam_ref[:, lanes(c)], (ns, V7X_LANES)) for c in range(nc)]
        li = [jnp.broadcast_to(lam_ref[:, lanes(nc + c)], (ns, V7X_LANES)) for c in range(nc)]

        def t_body(t, carry, g0=g0, ns=ns, lr=lr, li=li):
            xr, xi = carry
            rows = pl.ds(g0 * seq_pitch + t, ns, stride=seq_pitch)
            nr, ni = [], []
            for c in range(nc):
                r = lr[c] * xr[c] - li[c] * xi[c] + bu_ref[c, rows, :]
                i = lr[c] * xi[c] + li[c] * xr[c] + bu_ref[nc + c, rows, :]
                xs_ref[c, rows, :] = r
                xs_ref[nc + c, rows, :] = i
                nr.append(r)
                ni.append(i)
            return tuple(nr), tuple(ni)

        init = (tuple(cre_ref[g0:g0 + ns, lanes(c)] for c in range(nc)),
                tuple(cim_ref[g0:g0 + ns, lanes(c)] for c in range(nc)))
        xr, xi = lax.fori_loop(0, seq_rows, t_body, init, unroll=8)
        for c in range(nc):
            cre_ref[g0:g0 + ns, lanes(c)] = xr[c]
            cim_ref[g0:g0 + ns, lanes(c)] = xi[c]

    for p, u_ref in enumerate(u_refs):
        xs = jnp.concatenate([xs_ref[c, p * part_pitch:p * part_pitch + pr, :].astype(BF16)
                              for c in range(2 * nc)], axis=1)
        y = jnp.dot(xs, wc_ref[...], preferred_element_type=F32) + d_ref[...] * u_ref[...]
        y_ref[p] = jax.nn.gelu(y)
    nre_ref[...] = cre_ref[...]
    nim_ref[...] = cim_ref[...]


def _s5_scan(z, row0, n_seq, seq_len, st_re, st_im, wb, wc, lam, d_flat, part_rows):
    nb, bw, sw = wb.shape
    hs = sw // 2
    if seq_len >= part_rows:
        n_parts, nsb, seq_rows = n_seq, n_seq, part_rows
        n_rb = seq_len // part_rows
        spb = n_rb
    else:
        n_parts, nsb, seq_rows = 1, part_rows // seq_len, seq_len
        n_rb = n_seq * seq_len // part_rows
        spb = 1
    n_sblk = n_rb // spb
    rb0 = row0 // part_rows
    part_stride = seq_len // part_rows if n_parts > 1 else 0
    st_re3 = st_re.reshape(n_sblk, nsb, nb * hs)
    st_im3 = st_im.reshape(n_sblk, nsb, nb * hs)
    part_pitch = part_rows + V7X_SUBLANES if n_parts > 1 else part_rows
    seq_pitch = part_pitch if n_parts > 1 else seq_rows
    body = functools.partial(_s5_body, n_parts=n_parts, nsb=nsb, seq_rows=seq_rows, spb=spb,
                             part_pitch=part_pitch, seq_pitch=seq_pitch)
    state_spec = pl.BlockSpec((None, nsb, hs), lambda j, r: (r // spb, 0, j))
    u_specs = [pl.BlockSpec((part_rows, bw),
                            functools.partial(lambda j, r, p: (rb0 + p * part_stride + r, j), p=p))
               for p in range(n_parts)]
    rows_per_part = n_seq * seq_len // n_parts
    y, nre, nim = pl.pallas_call(
        body,
        grid=(nb, n_rb),
        in_specs=u_specs + [
            state_spec, state_spec,
            pl.BlockSpec((None, bw, sw), lambda j, r: (j, 0, 0)),
            pl.BlockSpec((None, sw, bw), lambda j, r: (j, 0, 0)),
            pl.BlockSpec((None, 1, sw), lambda j, r: (j, 0, 0)),
            pl.BlockSpec((1, bw), lambda j, r: (0, j))],
        out_specs=[pl.BlockSpec((n_parts, part_rows, bw), lambda j, r: (0, r, j)),
                   state_spec, state_spec],
        out_shape=[jax.ShapeDtypeStruct((n_parts, rows_per_part, nb * bw), F32),
                   jax.ShapeDtypeStruct((n_sblk, nsb, nb * hs), F32),
                   jax.ShapeDtypeStruct((n_sblk, nsb, nb * hs), F32)],
        scratch_shapes=[pltpu.VMEM((sw // V7X_LANES, n_parts * part_pitch, V7X_LANES), F32),
                        pltpu.VMEM((sw // V7X_LANES, n_parts * part_pitch, V7X_LANES), F32),
                        pltpu.VMEM((nsb, hs), F32),
                        pltpu.VMEM((nsb, hs), F32)],
        compiler_params=_cparams(("arbitrary", "arbitrary")),
        name="s5_scan",
    )(*([z] * n_parts), st_re3, st_im3, wb, wc, lam, d_flat)
    return (y.reshape(n_seq * seq_len, nb * bw),
            nre.reshape(n_seq, nb * hs), nim.reshape(n_seq, nb * hs))


def _cumsum_rows(x, seg):
    row = lax.broadcasted_iota(jnp.int32, x.shape, 0) % seg
    s = 1
    while s < seg:
        x = x + jnp.where(row >= s, pltpu.roll(x, s, axis=0), 0.0)
        s *= 2
    return x


def _hgrn_body(q_ref, f_ref, v_ref, g_ref, st_ref, lb_ref, ng_ref, o_ref, ns_ref, s_ref,
               *, rows, prows, n_chunks, n_heads):
    kd = HGRN_HEAD_DIM
    c = pl.program_id(1)

    @pl.when(c == 0)
    def _():
        s_ref[...] = st_ref[...]

    lb = lb_ref[...]
    fz = f_ref[...]
    q = q_ref[...]
    log_f = jnp.log(lb + (1.0 - lb) * _sigmoid(fz))
    kk = (1.0 - lb) * _sigmoid(-fz)
    qh = q * _sigmoid(q) * (kd ** -0.5)
    v = v_ref[...]
    if prows > rows:
        pad = jnp.zeros((prows - rows, fz.shape[1]), F32)
        log_f, kk, qh, v = [jnp.concatenate([a, pad], axis=0) for a in (log_f, kk, qh, v)]
    b = _cumsum_rows(log_f, prows)
    b_end = b[rows - 1:rows, :]
    b_mid = b[rows // 2 - 1:rows // 2, :]
    qt = (qh * jnp.exp(jnp.minimum(b - b_mid, EXP_CLAMP))).astype(BF16)
    kt = (kk * jnp.exp(jnp.minimum(b_mid - b, EXP_CLAMP))).astype(BF16)
    qe = (qh * jnp.exp(b)).astype(BF16)
    ke = (kk * jnp.exp(b_end - b)).astype(BF16)
    dec = jnp.exp(b_end)
    vb = v.astype(BF16)
    causal = (lax.broadcasted_iota(jnp.int32, (prows, prows), 0)
              >= lax.broadcasted_iota(jnp.int32, (prows, prows), 1))
    eye = (lax.broadcasted_iota(jnp.int32, (kd, kd), 0)
           == lax.broadcasted_iota(jnp.int32, (kd, kd), 1))
    og = g_ref[...]
    gate = og * _sigmoid(og)
    ng = ng_ref[...]
    for h in range(n_heads):
        sl = slice(h * kd, (h + 1) * kd)
        sc = lax.dot_general(qt[:, sl], kt[:, sl], (((1,), (1,)), ((), ())),
                             preferred_element_type=F32)
        sc = jnp.where(causal, sc, 0.0).astype(BF16)
        s_prev = s_ref[h]
        o = (jnp.dot(sc, vb[:, sl], preferred_element_type=F32)
             + jnp.dot(qe[:, sl], s_prev.astype(BF16), preferred_element_type=F32))
        upd = lax.dot_general(ke[:, sl], vb[:, sl], (((0,), (0,)), ((), ())),
                              preferred_element_type=F32)
        dec_col = jnp.sum(jnp.where(eye, jnp.broadcast_to(dec[:, sl], (kd, kd)), 0.0),
                          axis=1, keepdims=True)
        s_ref[h] = dec_col * s_prev + upd
        o = o[:rows]
        o = o * lax.rsqrt(jnp.mean(o * o, axis=-1, keepdims=True) + RMS_EPS) * ng
        o_ref[:, sl] = o * gate[:, sl]

    @pl.when(c == n_chunks - 1)
    def _():
        ns_ref[...] = s_ref[...]


def _hgrn(z, row0, n_seq, seq_len, state, lb_flat, norm_g, col_block0):
    n_heads, kd = state.shape[1], state.shape[2]
    dh = n_heads * kd
    rows = min(HGRN_CHUNK, seq_len)
    prows = max(rows, V7X_BF16_SUBLANES)
    n_chunks = seq_len // rows
    rb0 = row0 // rows
    body = functools.partial(_hgrn_body, rows=rows, prows=prows, n_chunks=n_chunks, n_heads=n_heads)

    def zspec(i):
        return pl.BlockSpec((rows, dh), lambda s, c: (rb0 + s * n_chunks + c, col_block0 + i))

    st_spec = pl.BlockSpec((None, n_heads, kd, kd), lambda s, c: (s, 0, 0, 0))
    o, ns = pl.pallas_call(
        body,
        grid=(n_seq, n_chunks),
        in_specs=[zspec(0), zspec(1), zspec(2), zspec(3), st_spec,
                  pl.BlockSpec((1, dh), lambda s, c: (0, 0)),
                  pl.BlockSpec((1, kd), lambda s, c: (0, 0))],
        out_specs=[pl.BlockSpec((rows, dh), lambda s, c: (s * n_chunks + c, 0)), st_spec],
        out_shape=[jax.ShapeDtypeStruct((n_seq * seq_len, dh), F32),
                   jax.ShapeDtypeStruct(state.shape, F32)],
        scratch_shapes=[pltpu.VMEM((n_heads, kd, kd), F32)],
        compiler_params=_cparams(("arbitrary", "arbitrary")),
        name="hgrn",
    )(z, z, z, z, state, lb_flat, norm_g.reshape(1, kd))
    return o, ns


def _router_body(x_ref, g_ref, w_ref, b_ref, xn_ref, idx_ref, gate_ref):
    x = x_ref[...]
    ms = jnp.mean(x * x, axis=-1, keepdims=True)
    xn = x * lax.rsqrt(ms + RMS_EPS) * g_ref[...]
    tm, d = xn.shape
    half = d // 2
    lo = lax.bitcast_convert_type(xn[:, :half].astype(BF16).astype(F32), jnp.uint32)
    hi = lax.bitcast_convert_type(xn[:, half:].astype(BF16).astype(F32), jnp.uint32)
    packed = (lo >> 16) | (hi & jnp.uint32(0xFFFF0000))
    wr = half // V7X_LANES
    for c in range(wr):
        xn_ref[pl.ds(c, tm, stride=wr), :] = packed[:, c * V7X_LANES:(c + 1) * V7X_LANES]
    logits = jnp.dot(xn.astype(BF16), w_ref[...].astype(BF16), preferred_element_type=F32) + b_ref[...]
    n_e = logits.shape[1]
    lane = lax.broadcasted_iota(jnp.int32, logits.shape, 1)
    out_lane = lax.broadcasted_iota(jnp.int32, idx_ref.shape, 1)
    idx_out = jnp.zeros(idx_ref.shape, jnp.int32)
    val_out = jnp.full(gate_ref.shape, -jnp.inf, F32)
    work = logits
    for k in range(TOP_K):
        m = jnp.max(work, axis=-1, keepdims=True)
        i = jnp.min(jnp.where(work == m, lane, n_e), axis=-1, keepdims=True)
        idx_out = jnp.where(out_lane == k, i, idx_out)
        val_out = jnp.where(out_lane == k, m, val_out)
        work = jnp.where(lane == i, -jnp.inf, work)
    top = jnp.max(val_out, axis=-1, keepdims=True)
    e = jnp.exp(val_out - top)
    gate_ref[...] = e / jnp.sum(e, axis=-1, keepdims=True)
    idx_ref[...] = idx_out


def _router(x, g, w_router, b_router, tm=256):
    t, d = x.shape
    n_e = w_router.shape[1]
    row = lambda i: (i, 0)
    fixed = lambda i: (0, 0)
    return pl.pallas_call(
        _router_body,
        grid=(t // tm,),
        in_specs=[pl.BlockSpec((tm, d), row), pl.BlockSpec((1, d), fixed),
                  pl.BlockSpec((d, n_e), fixed), pl.BlockSpec((1, n_e), fixed)],
        out_specs=[pl.BlockSpec((tm * (d // 2 // V7X_LANES), V7X_LANES), row),
                   pl.BlockSpec((tm, V7X_LANES), row), pl.BlockSpec((tm, V7X_LANES), row)],
        out_shape=[jax.ShapeDtypeStruct((t * (d // 2 // V7X_LANES), V7X_LANES), jnp.uint32),
                   jax.ShapeDtypeStruct((t, V7X_LANES), jnp.int32),
                   jax.ShapeDtypeStruct((t, V7X_LANES), F32)],
        compiler_params=_cparams(("arbitrary",)),
        name="router",
    )(x, g.reshape(1, d), w_router, b_router.reshape(1, n_e))


def _gather_rows_copy(src_hbm, buf_ref, sem_ref, tok, slot, r, wr):
    return pltpu.make_async_copy(src_hbm.at[pl.ds(pl.multiple_of(tok * wr, wr), wr), :],
                                 buf_ref.at[slot, pl.ds(pl.multiple_of(r * wr, wr), wr), :],
                                 sem_ref.at[slot])


def _gather_body(tok_ref, x_hbm, o_ref, buf_ref, sem_ref, *, tm, wr):
    i = pl.program_id(0)
    n = pl.num_programs(0)

    def issue(block, slot):
        def body(r, carry):
            _gather_rows_copy(x_hbm, buf_ref, sem_ref, tok_ref[block * tm + r], slot, r, wr).start()
            return carry
        lax.fori_loop(0, tm, body, 0, unroll=8)

    @pl.when(i == 0)
    def _():
        issue(0, 0)

    @pl.when(i + 1 < n)
    def _():
        issue(i + 1, (i + 1) % 2)

    slot = i % 2
    pltpu.make_async_copy(x_hbm.at[pl.ds(0, tm * wr), :], buf_ref.at[slot], sem_ref.at[slot]).wait()
    half = wr * V7X_LANES
    for c in range(wr):
        w = buf_ref[slot, pl.ds(c, tm, stride=wr), :]
        lanes = slice(c * V7X_LANES, (c + 1) * V7X_LANES)
        o_ref[:, lanes] = lax.bitcast_convert_type(w << 16, F32).astype(BF16)
        o_ref[:, half + c * V7X_LANES:half + (c + 1) * V7X_LANES] = (
            lax.bitcast_convert_type(w & jnp.uint32(0xFFFF0000), F32).astype(BF16))


def _gather_rows(x, n_tok, tok, tm):
    r_total = tok.shape[0]
    wr = x.shape[0] // n_tok
    d = 2 * wr * V7X_LANES
    return pl.pallas_call(
        functools.partial(_gather_body, tm=tm, wr=wr),
        grid_spec=pltpu.PrefetchScalarGridSpec(
            num_scalar_prefetch=1,
            grid=(r_total // tm,),
            in_specs=[pl.BlockSpec(memory_space=pl.ANY)],
            out_specs=pl.BlockSpec((tm, d), lambda i, tok: (i, 0)),
            scratch_shapes=[pltpu.VMEM((2, tm * wr, V7X_LANES), x.dtype),
                            pltpu.SemaphoreType.DMA((2,))]),
        out_shape=jax.ShapeDtypeStruct((r_total, d), BF16),
        compiler_params=_cparams(("arbitrary",)),
        name="moe_gather",
    )(tok, x)


def _swiglu_chunk(h, sel):
    gate = jnp.minimum(h, SWIGLU_LIMIT)
    up = jnp.clip(h, -SWIGLU_LIMIT, SWIGLU_LIMIT)
    glu = gate * _sigmoid(SWIGLU_ALPHA * gate)
    halves = []
    for c in range(2):
        sl = slice(c * V7X_LANES, (c + 1) * V7X_LANES)
        up_next = pltpu.roll(up[:, sl], V7X_LANES - 1, axis=1)
        halves.append(((up_next + 1.0) * glu[:, sl]).astype(BF16))
    prod = jnp.concatenate(halves, axis=1)
    return jnp.dot(prod, sel, preferred_element_type=F32).astype(BF16)


def _expert_mm_body(bstart_ref, nblk_ref, x_hbm, w_ref, b_ref, *rest, tm, swiglu, n_blocks):
    if swiglu:
        sel_ref, o_hbm, wbf_ref, xbuf, obuf, xsem, osem = rest
    else:
        o_hbm, wbf_ref, xbuf, obuf, xsem, osem = rest
    n = pl.program_id(0)
    e = pl.program_id(1)
    nb = nblk_ref[e]
    b0 = bstart_ref[e]
    otn = obuf.shape[2]
    col0 = pl.multiple_of(n * otn, otn)
    chunk = 2 * V7X_LANES

    def x_copy(blk, slot):
        r = pl.multiple_of(blk * tm, tm)
        return pltpu.make_async_copy(x_hbm.at[pl.ds(r, tm), :], xbuf.at[slot], xsem.at[slot])

    def o_copy(blk, slot):
        r = pl.multiple_of(blk * tm, tm)
        return pltpu.make_async_copy(obuf.at[slot], o_hbm.at[pl.ds(r, tm), pl.ds(col0, otn)],
                                     osem.at[slot])

    @pl.when(nb > 0)
    def _():
        x_copy(b0, 0).start(priority=1)
        _cast_weight(w_ref, wbf_ref)

    def body(i, carry):
        slot = i % 2
        x_copy(b0 + i, slot).wait()

        @pl.when(i + 1 < nb)
        def _():
            x_copy(b0 + i + 1, 1 - slot).start(priority=1)

        @pl.when(i >= 2)
        def _():
            o_copy(b0 + i - 2, slot).wait()

        x = xbuf[slot]
        n_chunks = wbf_ref.shape[1] // chunk

        def column_chunk(c):
            cs = slice(c * chunk, (c + 1) * chunk)
            return jnp.dot(x, wbf_ref[:, cs], preferred_element_type=F32) + b_ref[:, cs]

        h = column_chunk(0)
        for c in range(n_chunks):
            h_next = column_chunk(c + 1) if c + 1 < n_chunks else None
            if swiglu:
                obuf[slot, :, c * V7X_LANES:(c + 1) * V7X_LANES] = _swiglu_chunk(h, sel_ref[...])
            else:
                obuf[slot, :, c * chunk:(c + 1) * chunk] = h
            h = h_next
        o_copy(b0 + i, slot).start()
        return carry

    lax.fori_loop(0, nb, body, 0)

    @pl.when(nb >= 2)
    def _():
        o_copy(b0 + nb - 2, nb % 2).wait()

    @pl.when(nb >= 1)
    def _():
        o_copy(b0 + nb - 1, (nb - 1) % 2).wait()

    @pl.when(e == pl.num_programs(1) - 1)
    def _():
        obuf[0] = jnp.zeros(obuf.shape[1:], obuf.dtype)

        def zstart(blk, carry):
            o_copy(blk, 0).start()
            return carry

        def zwait(blk, carry):
            o_copy(blk, 0).wait()
            return carry

        lax.fori_loop(b0 + nb, n_blocks, zstart, 0)
        lax.fori_loop(b0 + nb, n_blocks, zwait, 0)


def _expert_mm(x, w, bias, bstart, nblk, tm, tn, swiglu):
    r_total, k = x.shape
    n_e, _, n_total = w.shape
    out_tn = tn // 2 if swiglu else tn
    out_n = n_total // 2 if swiglu else n_total
    out_dtype = BF16 if swiglu else F32
    in_specs = [pl.BlockSpec(memory_space=pl.ANY),
                pl.BlockSpec((None, k, tn), lambda n, e, bs, nb: (e, 0, n)),
                pl.BlockSpec((None, 1, tn), lambda n, e, bs, nb: (e, 0, n))]
    args = [x, w, bias.reshape(n_e, 1, n_total)]
    if swiglu:
        sel = (lax.broadcasted_iota(jnp.int32, (2 * V7X_LANES, V7X_LANES), 0)
               == 2 * lax.broadcasted_iota(jnp.int32, (2 * V7X_LANES, V7X_LANES), 1)).astype(BF16)
        in_specs.append(pl.BlockSpec((2 * V7X_LANES, V7X_LANES), lambda n, e, bs, nb: (0, 0)))
        args.append(sel)
    return pl.pallas_call(
        functools.partial(_expert_mm_body, tm=tm, swiglu=swiglu, n_blocks=r_total // tm),
        grid_spec=pltpu.PrefetchScalarGridSpec(
            num_scalar_prefetch=2,
            grid=(n_total // tn, n_e),
            in_specs=in_specs,
            out_specs=pl.BlockSpec(memory_space=pl.ANY),
            scratch_shapes=[pltpu.VMEM((k, tn), BF16),
                            pltpu.VMEM((2, tm, k), BF16),
                            pltpu.VMEM((2, tm, out_tn), out_dtype),
                            pltpu.SemaphoreType.DMA((2,)),
                            pltpu.SemaphoreType.DMA((2,))]),
        out_shape=jax.ShapeDtypeStruct((r_total, out_n), out_dtype),
        compiler_params=_cparams(("arbitrary", "arbitrary")),
        name="moe_gate_up" if swiglu else "moe_down",
    )(bstart, nblk, *args)


def _combine_rows_copy(src_hbm, buf_ref, sem_ref, row, slot, k, r):
    return pltpu.make_async_copy(src_hbm.at[pl.ds(row, 1), :],
                                 buf_ref.at[slot, k, pl.ds(r, 1), :],
                                 sem_ref.at[slot])


def _combine_body(dest_ref, yb_hbm, x_ref, gate_ref, g_ref, o_ref, buf_ref, sem_ref, *, tm):
    i = pl.program_id(0)
    n = pl.num_programs(0)

    def issue(block, slot):
        def body(r, carry):
            for k in range(TOP_K):
                row = dest_ref[(block * tm + r) * TOP_K + k]
                _combine_rows_copy(yb_hbm, buf_ref, sem_ref, row, slot, k, r).start()
            return carry
        lax.fori_loop(0, tm, body, 0, unroll=2)

    @pl.when(i == 0)
    def _():
        issue(0, 0)

    @pl.when(i + 1 < n)
    def _():
        issue(i + 1, (i + 1) % 2)

    slot = i % 2
    for k in range(TOP_K):
        pltpu.make_async_copy(yb_hbm.at[pl.ds(0, tm), :], buf_ref.at[slot, k], sem_ref.at[slot]).wait()
    gates = gate_ref[...]
    acc = x_ref[...]
    for k in range(TOP_K):
        acc = acc + gates[:, k:k + 1] * buf_ref[slot, k]
    ms = jnp.mean(acc * acc, axis=-1, keepdims=True)
    o_ref[...] = acc * lax.rsqrt(ms + RMS_EPS) * g_ref[...]


def _combine(yb, dest_flat, x, gates, g, tm):
    t, d = x.shape
    return pl.pallas_call(
        functools.partial(_combine_body, tm=tm),
        grid_spec=pltpu.PrefetchScalarGridSpec(
            num_scalar_prefetch=1,
            grid=(t // tm,),
            in_specs=[pl.BlockSpec(memory_space=pl.ANY),
                      pl.BlockSpec((tm, d), lambda i, dest: (i, 0)),
                      pl.BlockSpec((tm, V7X_LANES), lambda i, dest: (i, 0)),
                      pl.BlockSpec((1, d), lambda i, dest: (0, 0))],
            out_specs=pl.BlockSpec((tm, d), lambda i, dest: (i, 0)),
            scratch_shapes=[pltpu.VMEM((2, TOP_K, tm, d), F32), pltpu.SemaphoreType.DMA((2,))]),
        out_shape=jax.ShapeDtypeStruct((t, d), F32),
        compiler_params=_cparams(("arbitrary",)),
        name="moe_combine",
    )(dest_flat, yb, x, gates, g.reshape(1, d))


def _routing_tables(top_idx, n_e, tm):
    t = top_idx.shape[0]
    p = t * TOP_K
    flat_e = top_idx.reshape(p)
    onehot = (flat_e[:, None] == jnp.arange(n_e, dtype=jnp.int32)[None, :]).astype(jnp.int32)
    rank = jnp.take_along_axis(jnp.cumsum(onehot, axis=0), flat_e[:, None], axis=1)[:, 0] - 1
    counts = jnp.sum(onehot, axis=0)
    pcounts = (counts + tm - 1) // tm * tm
    pends = jnp.cumsum(pcounts)
    pstarts = pends - pcounts
    dest = (pstarts[flat_e] + rank).astype(jnp.int32)
    n_blocks = p // tm + n_e
    flat_tok = jnp.arange(p, dtype=jnp.int32) // TOP_K
    slot_tok = jnp.zeros((n_blocks * tm,), jnp.int32).at[dest].set(flat_tok)
    return dest, slot_tok, (pstarts // tm).astype(jnp.int32), (pcounts // tm).astype(jnp.int32)


def kernel(x_prompt, x_sample, state_s5_re, state_s5_im, state_hgrn, norm_mix, w_in, b_gate, s5_lam_re, s5_lam_im, s5_log_step, s5_b_re, s5_b_im, s5_c_re, s5_c_im, s5_d, s5_w_glu, hgrn_lb_logits, hgrn_norm, w_branch_s5, w_branch_hgrn, w_out, norm_ffn, w_router, b_router, w_gate_up, b_gate_up, w_down, b_down, norm_final):
    depth = w_in.shape[0]
    assert depth == 1, "single-layer stack"
    n_p, s_p, d = x_prompt.shape
    n_s, s_s, _ = x_sample.shape
    t_p, t_s = n_p * s_p, n_s * s_s
    g_s5, p_s5 = s5_lam_re.shape[1:]
    d_s5 = g_s5 * S5_GROUP
    n_heads = state_hgrn.shape[2]
    d_h = n_heads * HGRN_HEAD_DIM
    n_e = w_router.shape[2]
    l = 0

    lb_all = jnp.cumsum(jax.nn.softmax(hgrn_lb_logits.astype(F32), axis=0), axis=0)[:depth]
    x = jnp.concatenate([x_prompt.reshape(t_p, d), x_sample.reshape(t_s, d)], axis=0)
    t = t_p + t_s

    h = _rmsnorm(x, norm_mix[l], BF16)
    z = _dense_mm([(h, 0)], [w_in[l]], [], lambda accs, ex: accs[0], F32, tm=1024, tn=512, name="in_proj")

    wb, wc, lam = _s5_params(s5_lam_re[l], s5_lam_im[l], s5_log_step[l], s5_b_re[l], s5_b_im[l],
                             s5_c_re[l], s5_c_im[l])
    d_flat = s5_d[l].reshape(1, d_s5)
    zeros_s5 = jnp.zeros((n_p, g_s5 * p_s5), F32)
    y_p, p_re, p_im = _s5_scan(z, 0, n_p, s_p, zeros_s5, zeros_s5, wb, wc, lam, d_flat, part_rows=256)
    y_s, s_re, s_im = _s5_scan(z, t_p, n_s, s_s, state_s5_re[l].reshape(n_s, -1),
                               state_s5_im[l].reshape(n_s, -1), wb, wc, lam, d_flat, part_rows=256)
    y = jnp.concatenate([y_p, y_s], axis=0)
    a = _dense_mm([(y, 0)], [s5_w_glu[l]],
                  [(y, (1024, 512), lambda n, m: (m, n))],
                  lambda accs, ex: ex[0] * _sigmoid(accs[0]), BF16, tm=1024, tn=512, name="s5_glu")

    lb_flat = lb_all[l].reshape(1, d_h)
    cb0 = d_s5 // d_h
    zeros_hg = jnp.zeros((n_p,) + state_hgrn.shape[2:], F32)
    o_p, p_hg = _hgrn(z, 0, n_p, s_p, zeros_hg, lb_flat, hgrn_norm[l], cb0)
    o_s, s_hg = _hgrn(z, t_p, n_s, s_s, state_hgrn[l], lb_flat, hgrn_norm[l], cb0)
    b = jnp.concatenate([o_p, o_s], axis=0)

    tn = 512
    ga_cb = (d_s5 + 4 * d_h) // tn
    gb_cb = ga_cb + d // tn
    bg = b_gate[l].reshape(1, 2 * d)
    merged = _dense_mm(
        [(a, 0), (b, 0)], [w_branch_s5[l], w_branch_hgrn[l]],
        [(z, (512, tn), lambda n, m: (m, ga_cb + n)),
         (z, (512, tn), lambda n, m: (m, gb_cb + n)),
         (bg, (1, tn), lambda n, m: (0, n)),
         (bg, (1, tn), lambda n, m: (0, d // tn + n))],
        lambda accs, ex: (_sigmoid(ex[0] + ex[2]) * accs[0] + _sigmoid(ex[1] + ex[3]) * accs[1]),
        BF16, tm=512, tn=tn, name="merge")
    x1 = _dense_mm([(merged, 0)], [w_out[l]], [(x, (1024, 512), lambda n, m: (m, n))],
                   lambda accs, ex: ex[0] + accs[0], F32, tm=1024, tn=512, name="out_proj")

    tm_e = 256
    xn, idx_wide, gate_wide = _router(x1, norm_ffn[l], w_router[l], b_router[l])
    top_idx = idx_wide[:, :TOP_K]
    dest, slot_tok, bstart, nblk = _routing_tables(top_idx, n_e, tm_e)
    xs = _gather_rows(xn, t, slot_tok, tm_e)
    act = _expert_mm(xs, w_gate_up[l], b_gate_up[l], bstart, nblk, tm_e, 1024, swiglu=True)
    yb = _expert_mm(act, w_down[l], b_down[l], bstart, nblk, tm_e, 1024, swiglu=False)
    y_out = _combine(yb, dest, x1, gate_wide, norm_final, tm=128)

    y_prompt = y_out[:t_p].reshape(n_p, s_p, d)
    y_sample = y_out[t_p:].reshape(n_s, s_s, d)
    return (y_prompt, y_sample,
            p_re.reshape(1, n_p, g_s5, p_s5), p_im.reshape(1, n_p, g_s5, p_s5), p_hg[None],
            s_re.reshape(1, n_s, g_s5, p_s5), s_im.reshape(1, n_s, g_s5, p_s5), s_hg[None])
```

```python
import functools

import jax
import jax.numpy as jnp
from jax import lax
from jax.experimental import pallas as pl
from jax.experimental.pallas import tpu as pltpu

F32 = jnp.float32
BF16 = jnp.bfloat16

RMS_EPS = 1e-5
S5_GROUP = 16
S5_STATE = 64
S5_GROUPS_PER_BLOCK = 16
HGRN_HEAD_DIM = 128
HGRN_CHUNK = 32
TOP_K = 4
SWIGLU_LIMIT = 7.0
SWIGLU_ALPHA = 1.702
EXP_CLAMP = 80.0

V7X_LANES = 128
V7X_SUBLANES = 8
V7X_BF16_SUBLANES = 16
V7X_VMEM_LIMIT = 56 * 1024 * 1024
CAST_ROWS = 256
MOE_WEIGHT_SLABS = 8
ISSUE_UNROLL = 8


def _cparams(semantics, vmem=V7X_VMEM_LIMIT):
    return pltpu.CompilerParams(dimension_semantics=semantics, vmem_limit_bytes=vmem)


def _sigmoid(x):
    return 1.0 / (1.0 + jnp.exp(-x))


def _cast_weight(w_ref, wbf_ref, row0=0):
    k = w_ref.shape[0]
    rows = min(CAST_ROWS, k)

    def body(i, carry):
        r = pl.multiple_of(i * rows, rows)
        wbf_ref[pl.ds(row0 + r, rows), :] = w_ref[pl.ds(r, rows), :].astype(BF16)
        return carry

    lax.fori_loop(0, k // rows, body, 0)


def _rmsnorm_body(x_ref, g_ref, o_ref):
    x = x_ref[...]
    ms = jnp.mean(x * x, axis=-1, keepdims=True)
    o_ref[...] = (x * lax.rsqrt(ms + RMS_EPS) * g_ref[...]).astype(o_ref.dtype)


def _rmsnorm(x, g, out_dtype, tm=256):
    t, d = x.shape
    return pl.pallas_call(
        _rmsnorm_body,
        grid=(t // tm,),
        in_specs=[pl.BlockSpec((tm, d), lambda i: (i, 0)),
                  pl.BlockSpec((1, d), lambda i: (0, 0))],
        out_specs=pl.BlockSpec((tm, d), lambda i: (i, 0)),
        out_shape=jax.ShapeDtypeStruct((t, d), out_dtype),
        compiler_params=_cparams(("arbitrary",)),
        name="rmsnorm",
    )(x, g.reshape(1, d))


def _mm_body(*refs, n_x, n_extra, epilogue):
    x_refs = refs[:n_x]
    w_refs = refs[n_x:2 * n_x]
    extra_refs = refs[2 * n_x:2 * n_x + n_extra]
    o_ref = refs[2 * n_x + n_extra]
    wbf_refs = refs[2 * n_x + n_extra + 1:]

    @pl.when(pl.program_id(1) == 0)
    def _():
        for w_ref, wbf_ref in zip(w_refs, wbf_refs):
            _cast_weight(w_ref, wbf_ref)

    accs = [jnp.dot(x_ref[...].astype(BF16), wbf_ref[...], preferred_element_type=F32)
            for x_ref, wbf_ref in zip(x_refs, wbf_refs)]
    o_ref[...] = epilogue(accs, [e[...] for e in extra_refs]).astype(o_ref.dtype)


def _dense_mm(xs, ws, extras, epilogue, out_dtype, tm, tn, name):
    m_total = xs[0][0].shape[0]
    n_total = ws[0].shape[1]
    in_specs = []
    for (x, cb), w in zip(xs, ws):
        in_specs.append(pl.BlockSpec((tm, w.shape[0]), functools.partial(lambda n, m, cb: (m, cb), cb=cb)))
    for w in ws:
        in_specs.append(pl.BlockSpec((w.shape[0], tn), lambda n, m: (0, n)))
    for _, bshape, imap in extras:
        in_specs.append(pl.BlockSpec(bshape, imap))
    body = functools.partial(_mm_body, n_x=len(xs), n_extra=len(extras), epilogue=epilogue)
    return pl.pallas_call(
        body,
        grid=(n_total // tn, m_total // tm),
        in_specs=in_specs,
        out_specs=pl.BlockSpec((tm, tn), lambda n, m: (m, n)),
        out_shape=jax.ShapeDtypeStruct((m_total, n_total), out_dtype),
        scratch_shapes=[pltpu.VMEM((w.shape[0], tn), BF16) for w in ws],
        compiler_params=_cparams(("arbitrary", "arbitrary")),
        name=name,
    )(*[x for x, _ in xs], *ws, *[e for e, _, _ in extras])


def _s5_params(lam_re, lam_im, log_step, b_re, b_im, c_re, c_im):
    g, p = lam_re.shape
    h = b_re.shape[-1]
    gb = S5_GROUPS_PER_BLOCK
    nb = g // gb
    dt = jnp.exp(log_step)[:, None]
    mag = jnp.exp(lam_re * dt)
    ang = lam_im * dt
    lbar_re, lbar_im = mag * jnp.cos(ang), mag * jnp.sin(ang)
    nr, ni = lbar_re - 1.0, lbar_im
    den = lam_re * lam_re + lam_im * lam_im
    fr = (nr * lam_re + ni * lam_im) / den
    fi = (ni * lam_re - nr * lam_im) / den
    bb_re = fr[:, :, None] * b_re - fi[:, :, None] * b_im
    bb_im = fr[:, :, None] * b_im + fi[:, :, None] * b_re
    eye = jnp.eye(gb, dtype=F32)

    def blockdiag_in(bb):
        t = bb.reshape(nb, gb, p, h)
        return jnp.einsum('jgph,gk->jghkp', t, eye).reshape(nb, gb * h, gb * p)

    def blockdiag_out(c):
        t = c.reshape(nb, gb, h, p)
        return jnp.einsum('jghp,gk->jgpkh', t, eye).reshape(nb, gb * p, gb * h)

    wb = jnp.concatenate([blockdiag_in(bb_re), blockdiag_in(bb_im)], axis=2).astype(BF16)
    wc = jnp.concatenate([blockdiag_out(c_re), -blockdiag_out(c_im)], axis=1).astype(BF16)
    lam = jnp.concatenate([lbar_re.reshape(nb, 1, gb * p), lbar_im.reshape(nb, 1, gb * p)], axis=2)
    return wb, wc, lam


def _s5_body(*refs, n_parts, nsb, seq_rows, spb, part_pitch, seq_pitch):
    u_refs = refs[:n_parts]
    (sre_ref, sim_ref, wb_ref, wc_ref, lam_ref, d_ref,
     y_ref, nre_ref, nim_ref, bu_ref, xs_ref, cre_ref, cim_ref) = refs[n_parts:]
    hs = cre_ref.shape[1]
    pr = u_refs[0].shape[0]
    nc = hs // V7X_LANES
    for p, u_ref in enumerate(u_refs):
        bu = jnp.dot(u_ref[...].astype(BF16), wb_ref[...], preferred_element_type=F32)
        for c in range(2 * nc):
            bu_ref[c, p * part_pitch:p * part_pitch + pr, :] = bu[:, c * V7X_LANES:(c + 1) * V7X_LANES]

    @pl.when(pl.program_id(1) % spb == 0)
    def _():
        cre_ref[...] = sre_ref[...]
        cim_ref[...] = sim_ref[...]

    def lanes(c):
        return slice(c * V7X_LANES, (c + 1) * V7X_LANES)

    for g0 in range(0, nsb, V7X_SUBLANES):
        ns = min(V7X_SUBLANES, nsb - g0)
        lr = [jnp.broadcast_to(lam_ref[:, lanes(c)], (ns, V7X_LANES)) for c in range(nc)]
        li = [jnp.broadcast_to(lam_ref[:, lanes(nc + c)], (ns, V7X_LANES)) for c in range(nc)]

        def t_body(t, carry, g0=g0, ns=ns, lr=lr, li=li):
            xr, xi = carry
            rows = pl.ds(g0 * seq_pitch + t, ns, stride=seq_pitch)
            nr, ni = [], []
            for c in range(nc):
                r = lr[c] * xr[c] - li[c] * xi[c] + bu_ref[c, rows, :]
                i = lr[c] * xi[c] + li[c] * xr[c] + bu_ref[nc + c, rows, :]
                xs_ref[c, rows, :] = r
                xs_ref[nc + c, rows, :] = i
                nr.append(r)
                ni.append(i)
            return tuple(nr), tuple(ni)

        init = (tuple(cre_ref[g0:g0 + ns, lanes(c)] for c in range(nc)),
                tuple(cim_ref[g0:g0 + ns, lanes(c)] for c in range(nc)))
        xr, xi = lax.fori_loop(0, seq_rows, t_body, init, unroll=8)
        for c in range(nc):
            cre_ref[g0:g0 + ns, lanes(c)] = xr[c]
            cim_ref[g0:g0 + ns, lanes(c)] = xi[c]

    for p, u_ref in enumerate(u_refs):
        xs = jnp.concatenate([xs_ref[c, p * part_pitch:p * part_pitch + pr, :].astype(BF16)
                              for c in range(2 * nc)], axis=1)
        y = jnp.dot(xs, wc_ref[...], preferred_element_type=F32) + d_ref[...] * u_ref[...]
        y_ref[p] = jax.nn.gelu(y)
    nre_ref[...] = cre_ref[...]
    nim_ref[...] = cim_ref[...]


def _s5_scan(z, row0, n_seq, seq_len, st_re, st_im, wb, wc, lam, d_flat, part_rows):
    nb, bw, sw = wb.shape
    hs = sw // 2
    if seq_len >= part_rows:
        n_parts, nsb, seq_rows = n_seq, n_seq, part_rows
        n_rb = seq_len // part_rows
        spb = n_rb
    else:
        n_parts, nsb, seq_rows = 1, part_rows // seq_len, seq_len
        n_rb = n_seq * seq_len // part_rows
        spb = 1
    n_sblk = n_rb // spb
    rb0 = row0 // part_rows
    part_stride = seq_len // part_rows if n_parts > 1 else 0
    st_re3 = st_re.reshape(n_sblk, nsb, nb * hs)
    st_im3 = st_im.reshape(n_sblk, nsb, nb * hs)
    part_pitch = part_rows + V7X_SUBLANES if n_parts > 1 else part_rows
    seq_pitch = part_pitch if n_parts > 1 else seq_rows
    body = functools.partial(_s5_body, n_parts=n_parts, nsb=nsb, seq_rows=seq_rows, spb=spb,
                             part_pitch=part_pitch, seq_pitch=seq_pitch)
    state_spec = pl.BlockSpec((None, nsb, hs), lambda j, r: (r // spb, 0, j))
    u_specs = [pl.BlockSpec((part_rows, bw),
                            functools.partial(lambda j, r, p: (rb0 + p * part_stride + r, j), p=p))
               for p in range(n_parts)]
    rows_per_part = n_seq * seq_len // n_parts
    y, nre, nim = pl.pallas_call(
        body,
        grid=(nb, n_rb),
        in_specs=u_specs + [
            state_spec, state_spec,
            pl.BlockSpec((None, bw, sw), lambda j, r: (j, 0, 0)),
            pl.BlockSpec((None, sw, bw), lambda j, r: (j, 0, 0)),
            pl.BlockSpec((None, 1, sw), lambda j, r: (j, 0, 0)),
            pl.BlockSpec((1, bw), lambda j, r: (0, j))],
        out_specs=[pl.BlockSpec((n_parts, part_rows, bw), lambda j, r: (0, r, j)),
                   state_spec, state_spec],
        out_shape=[jax.ShapeDtypeStruct((n_parts, rows_per_part, nb * bw), F32),
                   jax.ShapeDtypeStruct((n_sblk, nsb, nb * hs), F32),
                   jax.ShapeDtypeStruct((n_sblk, nsb, nb * hs), F32)],
        scratch_shapes=[pltpu.VMEM((sw // V7X_LANES, n_parts * part_pitch, V7X_LANES), F32),
                        pltpu.VMEM((sw // V7X_LANES, n_parts * part_pitch, V7X_LANES), F32),
                        pltpu.VMEM((nsb, hs), F32),
                        pltpu.VMEM((nsb, hs), F32)],
        compiler_params=_cparams(("arbitrary", "arbitrary")),
        name="s5_scan",
    )(*([z] * n_parts), st_re3, st_im3, wb, wc, lam, d_flat)
    return (y.reshape(n_seq * seq_len, nb * bw),
            nre.reshape(n_seq, nb * hs), nim.reshape(n_seq, nb * hs))


def _cumsum_rows(x, seg):
    row = lax.broadcasted_iota(jnp.int32, x.shape, 0) % seg
    s = 1
    while s < seg:
        x = x + jnp.where(row >= s, pltpu.roll(x, s, axis=0), 0.0)
        s *= 2
    return x


def _hgrn_body(q_ref, f_ref, v_ref, g_ref, st_ref, lb_ref, ng_ref, o_ref, ns_ref, s_ref,
               *, rows, prows, n_chunks, n_heads):
    kd = HGRN_HEAD_DIM
    c = pl.program_id(1)

    @pl.when(c == 0)
    def _():
        s_ref[...] = st_ref[...]

    lb = lb_ref[...]
    fz = f_ref[...]
    q = q_ref[...]
    log_f = jnp.log(lb + (1.0 - lb) * _sigmoid(fz))
    kk = (1.0 - lb) * _sigmoid(-fz)
    qh = q * _sigmoid(q) * (kd ** -0.5)
    v = v_ref[...]
    if prows > rows:
        pad = jnp.zeros((prows - rows, fz.shape[1]), F32)
        log_f, kk, qh, v = [jnp.concatenate([a, pad], axis=0) for a in (log_f, kk, qh, v)]
    b = _cumsum_rows(log_f, prows)
    b_end = b[rows - 1:rows, :]
    b_mid = b[rows // 2 - 1:rows // 2, :]
    qt = (qh * jnp.exp(jnp.minimum(b - b_mid, EXP_CLAMP))).astype(BF16)
    kt = (kk * jnp.exp(jnp.minimum(b_mid - b, EXP_CLAMP))).astype(BF16)
    qe = (qh * jnp.exp(b)).astype(BF16)
    ke = (kk * jnp.exp(b_end - b)).astype(BF16)
    dec = jnp.exp(b_end)
    vb = v.astype(BF16)
    causal = (lax.broadcasted_iota(jnp.int32, (prows, prows), 0)
              >= lax.broadcasted_iota(jnp.int32, (prows, prows), 1))
    eye = (lax.broadcasted_iota(jnp.int32, (kd, kd), 0)
           == lax.broadcasted_iota(jnp.int32, (kd, kd), 1))
    og = g_ref[...]
    gate = og * _sigmoid(og)
    ng = ng_ref[...]
    for h in range(n_heads):
        sl = slice(h * kd, (h + 1) * kd)
        sc = lax.dot_general(qt[:, sl], kt[:, sl], (((1,), (1,)), ((), ())),
                             preferred_element_type=F32)
        sc = jnp.where(causal, sc, 0.0).astype(BF16)
        s_prev = s_ref[h]
        o = (jnp.dot(sc, vb[:, sl], preferred_element_type=F32)
             + jnp.dot(qe[:, sl], s_prev.astype(BF16), preferred_element_type=F32))
        upd = lax.dot_general(ke[:, sl], vb[:, sl], (((0,), (0,)), ((), ())),
                              preferred_element_type=F32)
        dec_col = jnp.sum(jnp.where(eye, jnp.broadcast_to(dec[:, sl], (kd, kd)), 0.0),
                          axis=1, keepdims=True)
        s_ref[h] = dec_col * s_prev + upd
        o = o[:rows]
        o = o * lax.rsqrt(jnp.mean(o * o, axis=-1, keepdims=True) + RMS_EPS) * ng
        o_ref[:, sl] = o * gate[:, sl]

    @pl.when(c == n_chunks - 1)
    def _():
        ns_ref[...] = s_ref[...]


def _hgrn(z, row0, n_seq, seq_len, state, lb_flat, norm_g, col_block0):
    n_heads, kd = state.shape[1], state.shape[2]
    dh = n_heads * kd
    rows = min(HGRN_CHUNK, seq_len)
    prows = max(rows, V7X_BF16_SUBLANES)
    n_chunks = seq_len // rows
    rb0 = row0 // rows
    body = functools.partial(_hgrn_body, rows=rows, prows=prows, n_chunks=n_chunks, n_heads=n_heads)

    def zspec(i):
        return pl.BlockSpec((rows, dh), lambda s, c: (rb0 + s * n_chunks + c, col_block0 + i))

    st_spec = pl.BlockSpec((None, n_heads, kd, kd), lambda s, c: (s, 0, 0, 0))
    o, ns = pl.pallas_call(
        body,
        grid=(n_seq, n_chunks),
        in_specs=[zspec(0), zspec(1), zspec(2), zspec(3), st_spec,
                  pl.BlockSpec((1, dh), lambda s, c: (0, 0)),
                  pl.BlockSpec((1, kd), lambda s, c: (0, 0))],
        out_specs=[pl.BlockSpec((rows, dh), lambda s, c: (s * n_chunks + c, 0)), st_spec],
        out_shape=[jax.ShapeDtypeStruct((n_seq * seq_len, dh), F32),
                   jax.ShapeDtypeStruct(state.shape, F32)],
        scratch_shapes=[pltpu.VMEM((n_heads, kd, kd), F32)],
        compiler_params=_cparams(("arbitrary", "arbitrary")),
        name="hgrn",
    )(z, z, z, z, state, lb_flat, norm_g.reshape(1, kd))
    return o, ns


def _router_body(x_ref, g_ref, w_ref, b_ref, xn_ref, idx_ref, gate_ref):
    x = x_ref[...]
    ms = jnp.mean(x * x, axis=-1, keepdims=True)
    xn = x * lax.rsqrt(ms + RMS_EPS) * g_ref[...]
    tm, d = xn.shape
    half = d // 2
    lo = lax.bitcast_convert_type(xn[:, :half].astype(BF16).astype(F32), jnp.uint32)
    hi = lax.bitcast_convert_type(xn[:, half:].astype(BF16).astype(F32), jnp.uint32)
    packed = (lo >> 16) | (hi & jnp.uint32(0xFFFF0000))
    wr = half // V7X_LANES
    for c in range(wr):
        xn_ref[pl.ds(c, tm, stride=wr), :] = packed[:, c * V7X_LANES:(c + 1) * V7X_LANES]
    logits = jnp.dot(xn.astype(BF16), w_ref[...].astype(BF16), preferred_element_type=F32) + b_ref[...]
    n_e = logits.shape[1]
    lane = lax.broadcasted_iota(jnp.int32, logits.shape, 1)
    out_lane = lax.broadcasted_iota(jnp.int32, idx_ref.shape, 1)
    idx_out = jnp.zeros(idx_ref.shape, jnp.int32)
    val_out = jnp.full(gate_ref.shape, -jnp.inf, F32)
    work = logits
    for k in range(TOP_K):
        m = jnp.max(work, axis=-1, keepdims=True)
        i = jnp.min(jnp.where(work == m, lane, n_e), axis=-1, keepdims=True)
        idx_out = jnp.where(out_lane == k, i, idx_out)
        val_out = jnp.where(out_lane == k, m, val_out)
        work = jnp.where(lane == i, -jnp.inf, work)
    top = jnp.max(val_out, axis=-1, keepdims=True)
    e = jnp.exp(val_out - top)
    gate_ref[...] = e / jnp.sum(e, axis=-1, keepdims=True)
    idx_ref[...] = idx_out


def _router(x, g, w_router, b_router, tm=256):
    t, d = x.shape
    n_e = w_router.shape[1]
    row = lambda i: (i, 0)
    fixed = lambda i: (0, 0)
    return pl.pallas_call(
        _router_body,
        grid=(t // tm,),
        in_specs=[pl.BlockSpec((tm, d), row), pl.BlockSpec((1, d), fixed),
                  pl.BlockSpec((d, n_e), fixed), pl.BlockSpec((1, n_e), fixed)],
        out_specs=[pl.BlockSpec((tm * (d // 2 // V7X_LANES), V7X_LANES), row),
                   pl.BlockSpec((tm, V7X_LANES), row), pl.BlockSpec((tm, V7X_LANES), row)],
        out_shape=[jax.ShapeDtypeStruct((t * (d // 2 // V7X_LANES), V7X_LANES), jnp.uint32),
                   jax.ShapeDtypeStruct((t, V7X_LANES), jnp.int32),
                   jax.ShapeDtypeStruct((t, V7X_LANES), F32)],
        compiler_params=_cparams(("arbitrary",)),
        name="router",
    )(x, g.reshape(1, d), w_router, b_router.reshape(1, n_e))


def _gather_rows_copy(src_hbm, buf_ref, sem_ref, tok, slot, r, wr):
    return pltpu.make_async_copy(src_hbm.at[pl.ds(pl.multiple_of(tok * wr, wr), wr), :],
                                 buf_ref.at[slot, pl.ds(pl.multiple_of(r * wr, wr), wr), :],
                                 sem_ref.at[slot])


def _gather_body(tok_ref, x_hbm, o_ref, buf_ref, sem_ref, *, tm, wr):
    i = pl.program_id(0)
    n = pl.num_programs(0)

    def issue(block, slot):
        def body(g, carry):
            for j in range(ISSUE_UNROLL):
                r = g * ISSUE_UNROLL + j
                _gather_rows_copy(x_hbm, buf_ref, sem_ref, tok_ref[block * tm + r], slot, r,
                                  wr).start(priority=j % 2)
            return carry
        lax.fori_loop(0, tm // ISSUE_UNROLL, body, 0)

    @pl.when(i == 0)
    def _():
        issue(0, 0)

    @pl.when(i + 1 < n)
    def _():
        issue(i + 1, (i + 1) % 2)

    slot = i % 2
    pltpu.make_async_copy(x_hbm.at[pl.ds(0, tm * wr), :], buf_ref.at[slot], sem_ref.at[slot]).wait()
    half = wr * V7X_LANES
    for c in range(wr):
        w = buf_ref[slot, pl.ds(c, tm, stride=wr), :]
        lanes = slice(c * V7X_LANES, (c + 1) * V7X_LANES)
        o_ref[:, lanes] = lax.bitcast_convert_type(w << 16, F32).astype(BF16)
        o_ref[:, half + c * V7X_LANES:half + (c + 1) * V7X_LANES] = (
            lax.bitcast_convert_type(w & jnp.uint32(0xFFFF0000), F32).astype(BF16))


def _gather_rows(x, n_tok, tok, tm):
    r_total = tok.shape[0]
    wr = x.shape[0] // n_tok
    d = 2 * wr * V7X_LANES
    return pl.pallas_call(
        functools.partial(_gather_body, tm=tm, wr=wr),
        grid_spec=pltpu.PrefetchScalarGridSpec(
            num_scalar_prefetch=1,
            grid=(r_total // tm,),
            in_specs=[pl.BlockSpec(memory_space=pl.ANY)],
            out_specs=pl.BlockSpec((tm, d), lambda i, tok: (i, 0)),
            scratch_shapes=[pltpu.VMEM((2, tm * wr, V7X_LANES), x.dtype),
                            pltpu.SemaphoreType.DMA((2,))]),
        out_shape=jax.ShapeDtypeStruct((r_total, d), BF16),
        compiler_params=_cparams(("arbitrary",)),
        name="moe_gather",
    )(tok, x)


def _swiglu_chunk(h, sel):
    gate = jnp.minimum(h, SWIGLU_LIMIT)
    up = jnp.clip(h, -SWIGLU_LIMIT, SWIGLU_LIMIT)
    glu = gate * _sigmoid(SWIGLU_ALPHA * gate)
    halves = []
    for c in range(2):
        sl = slice(c * V7X_LANES, (c + 1) * V7X_LANES)
        up_next = pltpu.roll(up[:, sl], V7X_LANES - 1, axis=1)
        halves.append(((up_next + 1.0) * glu[:, sl]).astype(BF16))
    prod = jnp.concatenate(halves, axis=1)
    return jnp.dot(prod, sel, preferred_element_type=F32).astype(BF16)


def _expert_mm_body(bstart_ref, nblk_ref, x_hbm, *rest, tm, swiglu, n_blocks, n_slabs):
    w_refs, b_ref, rest = rest[:n_slabs], rest[n_slabs], rest[n_slabs + 1:]
    if swiglu:
        sel_ref, o_hbm, wbf_ref, xbuf, obuf, xsem, osem = rest
    else:
        o_hbm, wbf_ref, xbuf, obuf, xsem, osem = rest
    n = pl.program_id(0)
    e = pl.program_id(1)
    nb = nblk_ref[e]
    b0 = bstart_ref[e]
    otn = obuf.shape[2]
    col0 = pl.multiple_of(n * otn, otn)
    chunk = 2 * V7X_LANES

    def x_copy(blk, slot):
        r = pl.multiple_of(blk * tm, tm)
        return pltpu.make_async_copy(x_hbm.at[pl.ds(r, tm), :], xbuf.at[slot], xsem.at[slot])

    def o_copy(blk, slot):
        r = pl.multiple_of(blk * tm, tm)
        return pltpu.make_async_copy(obuf.at[slot], o_hbm.at[pl.ds(r, tm), pl.ds(col0, otn)],
                                     osem.at[slot])

    @pl.when(nb > 0)
    def _():
        x_copy(b0, 0).start(priority=1)
        for s, w_ref in enumerate(w_refs):
            _cast_weight(w_ref, wbf_ref, s * w_ref.shape[0])

    def body(i, carry):
        slot = i % 2
        x_copy(b0 + i, slot).wait()

        @pl.when(i + 1 < nb)
        def _():
            x_copy(b0 + i + 1, 1 - slot).start(priority=1)

        @pl.when(i >= 2)
        def _():
            o_copy(b0 + i - 2, slot).wait()

        x = xbuf[slot]
        n_chunks = wbf_ref.shape[1] // chunk

        def column_chunk(c):
            cs = slice(c * chunk, (c + 1) * chunk)
            return jnp.dot(x, wbf_ref[:, cs], preferred_element_type=F32) + b_ref[:, cs]

        h = column_chunk(0)
        for c in range(n_chunks):
            h_next = column_chunk(c + 1) if c + 1 < n_chunks else None
            if swiglu:
                obuf[slot, :, c * V7X_LANES:(c + 1) * V7X_LANES] = _swiglu_chunk(h, sel_ref[...])
            else:
                obuf[slot, :, c * chunk:(c + 1) * chunk] = h
            h = h_next
        o_copy(b0 + i, slot).start()
        return carry

    lax.fori_loop(0, nb, body, 0)

    @pl.when(nb >= 2)
    def _():
        o_copy(b0 + nb - 2, nb % 2).wait()

    @pl.when(nb >= 1)
    def _():
        o_copy(b0 + nb - 1, (nb - 1) % 2).wait()

    @pl.when(e == pl.num_programs(1) - 1)
    def _():
        obuf[0] = jnp.zeros(obuf.shape[1:], obuf.dtype)

        def zstart(blk, carry):
            o_copy(blk, 0).start()
            return carry

        def zwait(blk, carry):
            o_copy(blk, 0).wait()
            return carry

        lax.fori_loop(b0 + nb, n_blocks, zstart, 0)
        lax.fori_loop(b0 + nb, n_blocks, zwait, 0)


def _expert_mm(x, w, bias, bstart, nblk, tm, tn, swiglu):
    r_total, k = x.shape
    n_e, _, n_total = w.shape
    out_tn = tn // 2 if swiglu else tn
    out_n = n_total // 2 if swiglu else n_total
    out_dtype = BF16 if swiglu else F32
    n_slabs = MOE_WEIGHT_SLABS
    in_specs = [pl.BlockSpec(memory_space=pl.ANY)]
    in_specs += [pl.BlockSpec((None, k // n_slabs, tn),
                              functools.partial(lambda n, e, bs, nb, s: (e, s, n), s=s))
                 for s in range(n_slabs)]
    in_specs.append(pl.BlockSpec((None, 1, tn), lambda n, e, bs, nb: (e, 0, n)))
    args = [x] + [w] * n_slabs + [bias.reshape(n_e, 1, n_total)]
    if swiglu:
        sel = (lax.broadcasted_iota(jnp.int32, (2 * V7X_LANES, V7X_LANES), 0)
               == 2 * lax.broadcasted_iota(jnp.int32, (2 * V7X_LANES, V7X_LANES), 1)).astype(BF16)
        in_specs.append(pl.BlockSpec((2 * V7X_LANES, V7X_LANES), lambda n, e, bs, nb: (0, 0)))
        args.append(sel)
    return pl.pallas_call(
        functools.partial(_expert_mm_body, tm=tm, swiglu=swiglu, n_blocks=r_total // tm,
                          n_slabs=n_slabs),
        grid_spec=pltpu.PrefetchScalarGridSpec(
            num_scalar_prefetch=2,
            grid=(n_total // tn, n_e),
            in_specs=in_specs,
            out_specs=pl.BlockSpec(memory_space=pl.ANY),
            scratch_shapes=[pltpu.VMEM((k, tn), BF16),
                            pltpu.VMEM((2, tm, k), BF16),
                            pltpu.VMEM((2, tm, out_tn), out_dtype),
                            pltpu.SemaphoreType.DMA((2,)),
                            pltpu.SemaphoreType.DMA((2,))]),
        out_shape=jax.ShapeDtypeStruct((r_total, out_n), out_dtype),
        compiler_params=_cparams(("arbitrary", "arbitrary")),
        name="moe_gate_up" if swiglu else "moe_down",
    )(bstart, nblk, *args)


def _combine_rows_copy(src_hbm, buf_ref, sem_ref, row, slot, k, r):
    return pltpu.make_async_copy(src_hbm.at[pl.ds(row, 1), :],
                                 buf_ref.at[slot, k, pl.ds(r, 1), :],
                                 sem_ref.at[slot])


def _combine_body(dest_ref, yb_hbm, x_ref, gate_ref, g_ref, o_ref, buf_ref, sem_ref, *, tm):
    i = pl.program_id(0)
    n = pl.num_programs(0)

    def issue(block, slot):
        def body(r, carry):
            for k in range(TOP_K):
                row = dest_ref[(block * tm + r) * TOP_K + k]
                _combine_rows_copy(yb_hbm, buf_ref, sem_ref, row, slot, k, r).start(priority=k % 2)
            return carry
        lax.fori_loop(0, tm, body, 0, unroll=2)

    @pl.when(i == 0)
    def _():
        issue(0, 0)

    @pl.when(i + 1 < n)
    def _():
        issue(i + 1, (i + 1) % 2)

    slot = i % 2
    for k in range(TOP_K):
        pltpu.make_async_copy(yb_hbm.at[pl.ds(0, tm), :], buf_ref.at[slot, k], sem_ref.at[slot]).wait()
    gates = gate_ref[...]
    acc = x_ref[...]
    for k in range(TOP_K):
        acc = acc + gates[:, k:k + 1] * buf_ref[slot, k]
    ms = jnp.mean(acc * acc, axis=-1, keepdims=True)
    o_ref[...] = acc * lax.rsqrt(ms + RMS_EPS) * g_ref[...]


def _combine(yb, dest_flat, x, gates, g, tm):
    t, d = x.shape
    return pl.pallas_call(
        functools.partial(_combine_body, tm=tm),
        grid_spec=pltpu.PrefetchScalarGridSpec(
            num_scalar_prefetch=1,
            grid=(t // tm,),
            in_specs=[pl.BlockSpec(memory_space=pl.ANY),
                      pl.BlockSpec((tm, d), lambda i, dest: (i, 0)),
                      pl.BlockSpec((tm, V7X_LANES), lambda i, dest: (i, 0)),
                      pl.BlockSpec((1, d), lambda i, dest: (0, 0))],
            out_specs=pl.BlockSpec((tm, d), lambda i, dest: (i, 0)),
            scratch_shapes=[pltpu.VMEM((2, TOP_K, tm, d), F32), pltpu.SemaphoreType.DMA((2,))]),
        out_shape=jax.ShapeDtypeStruct((t, d), F32),
        compiler_params=_cparams(("arbitrary",)),
        name="moe_combine",
    )(dest_flat, yb, x, gates, g.reshape(1, d))


def _routing_tables(top_idx, n_e, tm):
    t = top_idx.shape[0]
    p = t * TOP_K
    flat_e = top_idx.reshape(p)
    onehot = (flat_e[:, None] == jnp.arange(n_e, dtype=jnp.int32)[None, :]).astype(jnp.int32)
    rank = jnp.take_along_axis(jnp.cumsum(onehot, axis=0), flat_e[:, None], axis=1)[:, 0] - 1
    counts = jnp.sum(onehot, axis=0)
    pcounts = (counts + tm - 1) // tm * tm
    pends = jnp.cumsum(pcounts)
    pstarts = pends - pcounts
    dest = (pstarts[flat_e] + rank).astype(jnp.int32)
    n_blocks = p // tm + n_e
    flat_tok = jnp.arange(p, dtype=jnp.int32) // TOP_K
    slot_tok = jnp.zeros((n_blocks * tm,), jnp.int32).at[dest].set(flat_tok)
    return dest, slot_tok, (pstarts // tm).astype(jnp.int32), (pcounts // tm).astype(jnp.int32)


def kernel(x_prompt, x_sample, state_s5_re, state_s5_im, state_hgrn, norm_mix, w_in, b_gate, s5_lam_re, s5_lam_im, s5_log_step, s5_b_re, s5_b_im, s5_c_re, s5_c_im, s5_d, s5_w_glu, hgrn_lb_logits, hgrn_norm, w_branch_s5, w_branch_hgrn, w_out, norm_ffn, w_router, b_router, w_gate_up, b_gate_up, w_down, b_down, norm_final):
    depth = w_in.shape[0]
    assert depth == 1, "single-layer stack"
    n_p, s_p, d = x_prompt.shape
    n_s, s_s, _ = x_sample.shape
    t_p, t_s = n_p * s_p, n_s * s_s
    g_s5, p_s5 = s5_lam_re.shape[1:]
    d_s5 = g_s5 * S5_GROUP
    n_heads = state_hgrn.shape[2]
    d_h = n_heads * HGRN_HEAD_DIM
    n_e = w_router.shape[2]
    l = 0

    lb_all = jnp.cumsum(jax.nn.softmax(hgrn_lb_logits.astype(F32), axis=0), axis=0)[:depth]
    x = jnp.concatenate([x_prompt.reshape(t_p, d), x_sample.reshape(t_s, d)], axis=0)
    t = t_p + t_s

    h = _rmsnorm(x, norm_mix[l], BF16)
    z = _dense_mm([(h, 0)], [w_in[l]], [], lambda accs, ex: accs[0], F32, tm=1024, tn=512, name="in_proj")

    wb, wc, lam = _s5_params(s5_lam_re[l], s5_lam_im[l], s5_log_step[l], s5_b_re[l], s5_b_im[l],
                             s5_c_re[l], s5_c_im[l])
    d_flat = s5_d[l].reshape(1, d_s5)
    zeros_s5 = jnp.zeros((n_p, g_s5 * p_s5), F32)
    y_p, p_re, p_im = _s5_scan(z, 0, n_p, s_p, zeros_s5, zeros_s5, wb, wc, lam, d_flat, part_rows=256)
    y_s, s_re, s_im = _s5_scan(z, t_p, n_s, s_s, state_s5_re[l].reshape(n_s, -1),
                               state_s5_im[l].reshape(n_s, -1), wb, wc, lam, d_flat, part_rows=256)
    y = jnp.concatenate([y_p, y_s], axis=0)
    a = _dense_mm([(y, 0)], [s5_w_glu[l]],
                  [(y, (1024, 512), lambda n, m: (m, n))],
                  lambda accs, ex: ex[0] * _sigmoid(accs[0]), BF16, tm=1024, tn=512, name="s5_glu")

    lb_flat = lb_all[l].reshape(1, d_h)
    cb0 = d_s5 // d_h
    zeros_hg = jnp.zeros((n_p,) + state_hgrn.shape[2:], F32)
    o_p, p_hg = _hgrn(z, 0, n_p, s_p, zeros_hg, lb_flat, hgrn_norm[l], cb0)
    o_s, s_hg = _hgrn(z, t_p, n_s, s_s, state_hgrn[l], lb_flat, hgrn_norm[l], cb0)
    b = jnp.concatenate([o_p, o_s], axis=0)

    tn = 512
    ga_cb = (d_s5 + 4 * d_h) // tn
    gb_cb = ga_cb + d // tn
    bg = b_gate[l].reshape(1, 2 * d)
    merged = _dense_mm(
        [(a, 0), (b, 0)], [w_branch_s5[l], w_branch_hgrn[l]],
        [(z, (512, tn), lambda n, m: (m, ga_cb + n)),
         (z, (512, tn), lambda n, m: (m, gb_cb + n)),
         (bg, (1, tn), lambda n, m: (0, n)),
         (bg, (1, tn), lambda n, m: (0, d // tn + n))],
        lambda accs, ex: (_sigmoid(ex[0] + ex[2]) * accs[0] + _sigmoid(ex[1] + ex[3]) * accs[1]),
        BF16, tm=512, tn=tn, name="merge")
    x1 = _dense_mm([(merged, 0)], [w_out[l]], [(x, (1024, 512), lambda n, m: (m, n))],
                   lambda accs, ex: ex[0] + accs[0], F32, tm=1024, tn=512, name="out_proj")

    tm_e = 256
    xn, idx_wide, gate_wide = _router(x1, norm_ffn[l], w_router[l], b_router[l])
    top_idx = idx_wide[:, :TOP_K]
    dest, slot_tok, bstart, nblk = _routing_tables(top_idx, n_e, tm_e)
    xs = _gather_rows(xn, t, slot_tok, tm_e)
    act = _expert_mm(xs, w_gate_up[l], b_gate_up[l], bstart, nblk, tm_e, 1024, swiglu=True)
    yb = _expert_mm(act, w_down[l], b_down[l], bstart, nblk, tm_e, 1024, swiglu=False)
    y_out = _combine(yb, dest, x1, gate_wide, norm_final, tm=128)

    y_prompt = y_out[:t_p].reshape(n_p, s_p, d)
    y_sample = y_out[t_p:].reshape(n_s, s_s, d)
    return (y_prompt, y_sample,
            p_re.reshape(1, n_p, g_s5, p_s5), p_im.reshape(1, n_p, g_s5, p_s5), p_hg[None],
            s_re.reshape(1, n_s, g_s5, p_s5), s_im.reshape(1, n_s, g_s5, p_s5), s_hg[None])
```

```python
import functools

import jax
import jax.numpy as jnp
from jax import lax
from jax.experimental import pallas as pl
from jax.experimental.pallas import tpu as pltpu

F32 = jnp.float32
BF16 = jnp.bfloat16

RMS_EPS = 1e-5
S5_GROUP = 16
S5_STATE = 64
S5_GROUPS_PER_BLOCK = 16
HGRN_HEAD_DIM = 128
HGRN_CHUNK = 32
TOP_K = 4
SWIGLU_LIMIT = 7.0
SWIGLU_ALPHA = 1.702
EXP_CLAMP = 80.0

V7X_LANES = 128
V7X_SUBLANES = 8
V7X_BF16_SUBLANES = 16
V7X_VMEM_LIMIT = 56 * 1024 * 1024
CAST_ROWS = 256
GROUP_BLOCKS = 5
ISSUE_UNROLL = 8


def _cparams(semantics, vmem=V7X_VMEM_LIMIT):
    return pltpu.CompilerParams(dimension_semantics=semantics, vmem_limit_bytes=vmem)


def _sigmoid(x):
    return 1.0 / (1.0 + jnp.exp(-x))


def _cast_weight(w_ref, wbf_ref, row0=0):
    k = w_ref.shape[0]
    rows = min(CAST_ROWS, k)

    def body(i, carry):
        r = pl.multiple_of(i * rows, rows)
        wbf_ref[pl.ds(row0 + r, rows), :] = w_ref[pl.ds(r, rows), :].astype(BF16)
        return carry

    lax.fori_loop(0, k // rows, body, 0)


def _rmsnorm_body(x_ref, g_ref, o_ref):
    x = x_ref[...]
    ms = jnp.mean(x * x, axis=-1, keepdims=True)
    o_ref[...] = (x * lax.rsqrt(ms + RMS_EPS) * g_ref[...]).astype(o_ref.dtype)


def _rmsnorm(x, g, out_dtype, tm=256):
    t, d = x.shape
    return pl.pallas_call(
        _rmsnorm_body,
        grid=(t // tm,),
        in_specs=[pl.BlockSpec((tm, d), lambda i: (i, 0)),
                  pl.BlockSpec((1, d), lambda i: (0, 0))],
        out_specs=pl.BlockSpec((tm, d), lambda i: (i, 0)),
        out_shape=jax.ShapeDtypeStruct((t, d), out_dtype),
        compiler_params=_cparams(("arbitrary",)),
        name="rmsnorm",
    )(x, g.reshape(1, d))


def _mm_body(*refs, n_x, n_extra, epilogue):
    x_refs = refs[:n_x]
    w_refs = refs[n_x:2 * n_x]
    extra_refs = refs[2 * n_x:2 * n_x + n_extra]
    o_ref = refs[2 * n_x + n_extra]
    wbf_refs = refs[2 * n_x + n_extra + 1:]

    @pl.when(pl.program_id(1) == 0)
    def _():
        for w_ref, wbf_ref in zip(w_refs, wbf_refs):
            _cast_weight(w_ref, wbf_ref)

    accs = [jnp.dot(x_ref[...].astype(BF16), wbf_ref[...], preferred_element_type=F32)
            for x_ref, wbf_ref in zip(x_refs, wbf_refs)]
    o_ref[...] = epilogue(accs, [e[...] for e in extra_refs]).astype(o_ref.dtype)


def _dense_mm(xs, ws, extras, epilogue, out_dtype, tm, tn, name):
    m_total = xs[0][0].shape[0]
    n_total = ws[0].shape[1]
    in_specs = []
    for (x, cb), w in zip(xs, ws):
        in_specs.append(pl.BlockSpec((tm, w.shape[0]), functools.partial(lambda n, m, cb: (m, cb), cb=cb)))
    for w in ws:
        in_specs.append(pl.BlockSpec((w.shape[0], tn), lambda n, m: (0, n)))
    for _, bshape, imap in extras:
        in_specs.append(pl.BlockSpec(bshape, imap))
    body = functools.partial(_mm_body, n_x=len(xs), n_extra=len(extras), epilogue=epilogue)
    return pl.pallas_call(
        body,
        grid=(n_total // tn, m_total // tm),
        in_specs=in_specs,
        out_specs=pl.BlockSpec((tm, tn), lambda n, m: (m, n)),
        out_shape=jax.ShapeDtypeStruct((m_total, n_total), out_dtype),
        scratch_shapes=[pltpu.VMEM((w.shape[0], tn), BF16) for w in ws],
        compiler_params=_cparams(("arbitrary", "arbitrary")),
        name=name,
    )(*[x for x, _ in xs], *ws, *[e for e, _, _ in extras])


def _s5_params(lam_re, lam_im, log_step, b_re, b_im, c_re, c_im):
    g, p = lam_re.shape
    h = b_re.shape[-1]
    gb = S5_GROUPS_PER_BLOCK
    nb = g // gb
    dt = jnp.exp(log_step)[:, None]
    mag = jnp.exp(lam_re * dt)
    ang = lam_im * dt
    lbar_re, lbar_im = mag * jnp.cos(ang), mag * jnp.sin(ang)
    nr, ni = lbar_re - 1.0, lbar_im
    den = lam_re * lam_re + lam_im * lam_im
    fr = (nr * lam_re + ni * lam_im) / den
    fi = (ni * lam_re - nr * lam_im) / den
    bb_re = fr[:, :, None] * b_re - fi[:, :, None] * b_im
    bb_im = fr[:, :, None] * b_im + fi[:, :, None] * b_re
    eye = jnp.eye(gb, dtype=F32)

    def blockdiag_in(bb):
        t = bb.reshape(nb, gb, p, h)
        return jnp.einsum('jgph,gk->jghkp', t, eye).reshape(nb, gb * h, gb * p)

    def blockdiag_out(c):
        t = c.reshape(nb, gb, h, p)
        return jnp.einsum('jghp,gk->jgpkh', t, eye).reshape(nb, gb * p, gb * h)

    wb = jnp.concatenate([blockdiag_in(bb_re), blockdiag_in(bb_im)], axis=2).astype(BF16)
    wc = jnp.concatenate([blockdiag_out(c_re), -blockdiag_out(c_im)], axis=1).astype(BF16)
    lam = jnp.concatenate([lbar_re.reshape(nb, 1, gb * p), lbar_im.reshape(nb, 1, gb * p)], axis=2)
    return wb, wc, lam


def _s5_body(*refs, n_parts, nsb, seq_rows, spb, part_pitch, seq_pitch):
    u_refs = refs[:n_parts]
    (sre_ref, sim_ref, wb_ref, wc_ref, lam_ref, d_ref,
     y_ref, nre_ref, nim_ref, bu_ref, xs_ref, cre_ref, cim_ref) = refs[n_parts:]
    hs = cre_ref.shape[1]
    pr = u_refs[0].shape[0]
    nc = hs // V7X_LANES
    for p, u_ref in enumerate(u_refs):
        bu = jnp.dot(u_ref[...].astype(BF16), wb_ref[...], preferred_element_type=F32)
        for c in range(2 * nc):
            bu_ref[c, p * part_pitch:p * part_pitch + pr, :] = bu[:, c * V7X_LANES:(c + 1) * V7X_LANES]

    @pl.when(pl.program_id(1) % spb == 0)
    def _():
        cre_ref[...] = sre_ref[...]
        cim_ref[...] = sim_ref[...]

    def lanes(c):
        return slice(c * V7X_LANES, (c + 1) * V7X_LANES)

    for g0 in range(0, nsb, V7X_SUBLANES):
        ns = min(V7X_SUBLANES, nsb - g0)
        lr = [jnp.broadcast_to(lam_ref[:, lanes(c)], (ns, V7X_LANES)) for c in range(nc)]
        li = [jnp.broadcast_to(lam_ref[:, lanes(nc + c)], (ns, V7X_LANES)) for c in range(nc)]

        def t_body(t, carry, g0=g0, ns=ns, lr=lr, li=li):
            xr, xi = carry
            rows = pl.ds(g0 * seq_pitch + t, ns, stride=seq_pitch)
            nr, ni = [], []
            for c in range(nc):
                r = lr[c] * xr[c] - li[c] * xi[c] + bu_ref[c, rows, :]
                i = lr[c] * xi[c] + li[c] * xr[c] + bu_ref[nc + c, rows, :]
                xs_ref[c, rows, :] = r
                xs_ref[nc + c, rows, :] = i
                nr.append(r)
                ni.append(i)
            return tuple(nr), tuple(ni)

        init = (tuple(cre_ref[g0:g0 + ns, lanes(c)] for c in range(nc)),
                tuple(cim_ref[g0:g0 + ns, lanes(c)] for c in range(nc)))
        xr, xi = lax.fori_loop(0, seq_rows, t_body, init, unroll=8)
        for c in range(nc):
            cre_ref[g0:g0 + ns, lanes(c)] = xr[c]
            cim_ref[g0:g0 + ns, lanes(c)] = xi[c]

    for p, u_ref in enumerate(u_refs):
        xs = jnp.concatenate([xs_ref[c, p * part_pitch:p * part_pitch + pr, :].astype(BF16)
                              for c in range(2 * nc)], axis=1)
        y = jnp.dot(xs, wc_ref[...], preferred_element_type=F32) + d_ref[...] * u_ref[...]
        y_ref[p] = jax.nn.gelu(y)
    nre_ref[...] = cre_ref[...]
    nim_ref[...] = cim_ref[...]


def _s5_scan(z, row0, n_seq, seq_len, st_re, st_im, wb, wc, lam, d_flat, part_rows):
    nb, bw, sw = wb.shape
    hs = sw // 2
    if seq_len >= part_rows:
        n_parts, nsb, seq_rows = n_seq, n_seq, part_rows
        n_rb = seq_len // part_rows
        spb = n_rb
    else:
        n_parts, nsb, seq_rows = 1, part_rows // seq_len, seq_len
        n_rb = n_seq * seq_len // part_rows
        spb = 1
    n_sblk = n_rb // spb
    rb0 = row0 // part_rows
    part_stride = seq_len // part_rows if n_parts > 1 else 0
    st_re3 = st_re.reshape(n_sblk, nsb, nb * hs)
    st_im3 = st_im.reshape(n_sblk, nsb, nb * hs)
    part_pitch = part_rows + V7X_SUBLANES if n_parts > 1 else part_rows
    seq_pitch = part_pitch if n_parts > 1 else seq_rows
    body = functools.partial(_s5_body, n_parts=n_parts, nsb=nsb, seq_rows=seq_rows, spb=spb,
                             part_pitch=part_pitch, seq_pitch=seq_pitch)
    state_spec = pl.BlockSpec((None, nsb, hs), lambda j, r: (r // spb, 0, j))
    u_specs = [pl.BlockSpec((part_rows, bw),
                            functools.partial(lambda j, r, p: (rb0 + p * part_stride + r, j), p=p))
               for p in range(n_parts)]
    rows_per_part = n_seq * seq_len // n_parts
    y, nre, nim = pl.pallas_call(
        body,
        grid=(nb, n_rb),
        in_specs=u_specs + [
            state_spec, state_spec,
            pl.BlockSpec((None, bw, sw), lambda j, r: (j, 0, 0)),
            pl.BlockSpec((None, sw, bw), lambda j, r: (j, 0, 0)),
            pl.BlockSpec((None, 1, sw), lambda j, r: (j, 0, 0)),
            pl.BlockSpec((1, bw), lambda j, r: (0, j))],
        out_specs=[pl.BlockSpec((n_parts, part_rows, bw), lambda j, r: (0, r, j)),
                   state_spec, state_spec],
        out_shape=[jax.ShapeDtypeStruct((n_parts, rows_per_part, nb * bw), F32),
                   jax.ShapeDtypeStruct((n_sblk, nsb, nb * hs), F32),
                   jax.ShapeDtypeStruct((n_sblk, nsb, nb * hs), F32)],
        scratch_shapes=[pltpu.VMEM((sw // V7X_LANES, n_parts * part_pitch, V7X_LANES), F32),
                        pltpu.VMEM((sw // V7X_LANES, n_parts * part_pitch, V7X_LANES), F32),
                        pltpu.VMEM((nsb, hs), F32),
                        pltpu.VMEM((nsb, hs), F32)],
        compiler_params=_cparams(("arbitrary", "arbitrary")),
        name="s5_scan",
    )(*([z] * n_parts), st_re3, st_im3, wb, wc, lam, d_flat)
    return (y.reshape(n_seq * seq_len, nb * bw),
            nre.reshape(n_seq, nb * hs), nim.reshape(n_seq, nb * hs))


def _cumsum_rows(x, seg):
    row = lax.broadcasted_iota(jnp.int32, x.shape, 0) % seg
    s = 1
    while s < seg:
        x = x + jnp.where(row >= s, pltpu.roll(x, s, axis=0), 0.0)
        s *= 2
    return x


def _hgrn_body(q_ref, f_ref, v_ref, g_ref, st_ref, lb_ref, ng_ref, o_ref, ns_ref, s_ref,
               *, rows, prows, n_chunks, n_heads):
    kd = HGRN_HEAD_DIM
    c = pl.program_id(1)

    @pl.when(c == 0)
    def _():
        s_ref[...] = st_ref[...]

    lb = lb_ref[...]
    fz = f_ref[...]
    q = q_ref[...]
    log_f = jnp.log(lb + (1.0 - lb) * _sigmoid(fz))
    kk = (1.0 - lb) * _sigmoid(-fz)
    qh = q * _sigmoid(q) * (kd ** -0.5)
    v = v_ref[...]
    if prows > rows:
        pad = jnp.zeros((prows - rows, fz.shape[1]), F32)
        log_f, kk, qh, v = [jnp.concatenate([a, pad], axis=0) for a in (log_f, kk, qh, v)]
    b = _cumsum_rows(log_f, prows)
    b_end = b[rows - 1:rows, :]
    b_mid = b[rows // 2 - 1:rows // 2, :]
    qt = (qh * jnp.exp(jnp.minimum(b - b_mid, EXP_CLAMP))).astype(BF16)
    kt = (kk * jnp.exp(jnp.minimum(b_mid - b, EXP_CLAMP))).astype(BF16)
    qe = (qh * jnp.exp(b)).astype(BF16)
    ke = (kk * jnp.exp(b_end - b)).astype(BF16)
    dec = jnp.exp(b_end)
    vb = v.astype(BF16)
    causal = (lax.broadcasted_iota(jnp.int32, (prows, prows), 0)
              >= lax.broadcasted_iota(jnp.int32, (prows, prows), 1))
    eye = (lax.broadcasted_iota(jnp.int32, (kd, kd), 0)
           == lax.broadcasted_iota(jnp.int32, (kd, kd), 1))
    og = g_ref[...]
    gate = og * _sigmoid(og)
    ng = ng_ref[...]
    for h in range(n_heads):
        sl = slice(h * kd, (h + 1) * kd)
        sc = lax.dot_general(qt[:, sl], kt[:, sl], (((1,), (1,)), ((), ())),
                             preferred_element_type=F32)
        sc = jnp.where(causal, sc, 0.0).astype(BF16)
        s_prev = s_ref[h]
        o = (jnp.dot(sc, vb[:, sl], preferred_element_type=F32)
             + jnp.dot(qe[:, sl], s_prev.astype(BF16), preferred_element_type=F32))
        upd = lax.dot_general(ke[:, sl], vb[:, sl], (((0,), (0,)), ((), ())),
                              preferred_element_type=F32)
        dec_col = jnp.sum(jnp.where(eye, jnp.broadcast_to(dec[:, sl], (kd, kd)), 0.0),
                          axis=1, keepdims=True)
        s_ref[h] = dec_col * s_prev + upd
        o = o[:rows]
        o = o * lax.rsqrt(jnp.mean(o * o, axis=-1, keepdims=True) + RMS_EPS) * ng
        o_ref[:, sl] = o * gate[:, sl]

    @pl.when(c == n_chunks - 1)
    def _():
        ns_ref[...] = s_ref[...]


def _hgrn(z, row0, n_seq, seq_len, state, lb_flat, norm_g, col_block0):
    n_heads, kd = state.shape[1], state.shape[2]
    dh = n_heads * kd
    rows = min(HGRN_CHUNK, seq_len)
    prows = max(rows, V7X_BF16_SUBLANES)
    n_chunks = seq_len // rows
    rb0 = row0 // rows
    body = functools.partial(_hgrn_body, rows=rows, prows=prows, n_chunks=n_chunks, n_heads=n_heads)

    def zspec(i):
        return pl.BlockSpec((rows, dh), lambda s, c: (rb0 + s * n_chunks + c, col_block0 + i))

    st_spec = pl.BlockSpec((None, n_heads, kd, kd), lambda s, c: (s, 0, 0, 0))
    o, ns = pl.pallas_call(
        body,
        grid=(n_seq, n_chunks),
        in_specs=[zspec(0), zspec(1), zspec(2), zspec(3), st_spec,
                  pl.BlockSpec((1, dh), lambda s, c: (0, 0)),
                  pl.BlockSpec((1, kd), lambda s, c: (0, 0))],
        out_specs=[pl.BlockSpec((rows, dh), lambda s, c: (s * n_chunks + c, 0)), st_spec],
        out_shape=[jax.ShapeDtypeStruct((n_seq * seq_len, dh), F32),
                   jax.ShapeDtypeStruct(state.shape, F32)],
        scratch_shapes=[pltpu.VMEM((n_heads, kd, kd), F32)],
        compiler_params=_cparams(("arbitrary", "arbitrary")),
        name="hgrn",
    )(z, z, z, z, state, lb_flat, norm_g.reshape(1, kd))
    return o, ns


def _router_body(x_ref, g_ref, w_ref, b_ref, xn_ref, idx_ref, gate_ref):
    x = x_ref[...]
    ms = jnp.mean(x * x, axis=-1, keepdims=True)
    xn = x * lax.rsqrt(ms + RMS_EPS) * g_ref[...]
    tm, d = xn.shape
    half = d // 2
    lo = lax.bitcast_convert_type(xn[:, :half].astype(BF16).astype(F32), jnp.uint32)
    hi = lax.bitcast_convert_type(xn[:, half:].astype(BF16).astype(F32), jnp.uint32)
    packed = (lo >> 16) | (hi & jnp.uint32(0xFFFF0000))
    wr = half // V7X_LANES
    for c in range(wr):
        xn_ref[pl.ds(c, tm, stride=wr), :] = packed[:, c * V7X_LANES:(c + 1) * V7X_LANES]
    logits = jnp.dot(xn.astype(BF16), w_ref[...].astype(BF16), preferred_element_type=F32) + b_ref[...]
    n_e = logits.shape[1]
    lane = lax.broadcasted_iota(jnp.int32, logits.shape, 1)
    out_lane = lax.broadcasted_iota(jnp.int32, idx_ref.shape, 1)
    idx_out = jnp.zeros(idx_ref.shape, jnp.int32)
    val_out = jnp.full(gate_ref.shape, -jnp.inf, F32)
    work = logits
    for k in range(TOP_K):
        m = jnp.max(work, axis=-1, keepdims=True)
        i = jnp.min(jnp.where(work == m, lane, n_e), axis=-1, keepdims=True)
        idx_out = jnp.where(out_lane == k, i, idx_out)
        val_out = jnp.where(out_lane == k, m, val_out)
        work = jnp.where(lane == i, -jnp.inf, work)
    top = jnp.max(val_out, axis=-1, keepdims=True)
    e = jnp.exp(val_out - top)
    gate_ref[...] = e / jnp.sum(e, axis=-1, keepdims=True)
    idx_ref[...] = idx_out


def _router(x, g, w_router, b_router, tm=256):
    t, d = x.shape
    n_e = w_router.shape[1]
    row = lambda i: (i, 0)
    fixed = lambda i: (0, 0)
    return pl.pallas_call(
        _router_body,
        grid=(t // tm,),
        in_specs=[pl.BlockSpec((tm, d), row), pl.BlockSpec((1, d), fixed),
                  pl.BlockSpec((d, n_e), fixed), pl.BlockSpec((1, n_e), fixed)],
        out_specs=[pl.BlockSpec((tm * (d // 2 // V7X_LANES), V7X_LANES), row),
                   pl.BlockSpec((tm, V7X_LANES), row), pl.BlockSpec((tm, V7X_LANES), row)],
        out_shape=[jax.ShapeDtypeStruct((t * (d // 2 // V7X_LANES), V7X_LANES), jnp.uint32),
                   jax.ShapeDtypeStruct((t, V7X_LANES), jnp.int32),
                   jax.ShapeDtypeStruct((t, V7X_LANES), F32)],
        compiler_params=_cparams(("arbitrary",)),
        name="router",
    )(x, g.reshape(1, d), w_router, b_router.reshape(1, n_e))


def _gather_rows_copy(src_hbm, buf_ref, sem_ref, tok, slot, r, wr):
    return pltpu.make_async_copy(src_hbm.at[pl.ds(pl.multiple_of(tok * wr, wr), wr), :],
                                 buf_ref.at[slot, pl.ds(pl.multiple_of(r * wr, wr), wr), :],
                                 sem_ref.at[slot])


def _gather_body(tok_ref, x_hbm, o_ref, buf_ref, sem_ref, *, tm, wr):
    i = pl.program_id(0)
    n = pl.num_programs(0)

    def issue(block, slot):
        def body(r, carry):
            _gather_rows_copy(x_hbm, buf_ref, sem_ref, tok_ref[block * tm + r], slot, r, wr).start()
            return carry
        lax.fori_loop(0, tm, body, 0, unroll=ISSUE_UNROLL)

    @pl.when(i == 0)
    def _():
        issue(0, 0)

    @pl.when(i + 1 < n)
    def _():
        issue(i + 1, (i + 1) % 2)

    slot = i % 2
    pltpu.make_async_copy(x_hbm.at[pl.ds(0, tm * wr), :], buf_ref.at[slot], sem_ref.at[slot]).wait()
    half = wr * V7X_LANES
    for c in range(wr):
        w = buf_ref[slot, pl.ds(c, tm, stride=wr), :]
        lanes = slice(c * V7X_LANES, (c + 1) * V7X_LANES)
        o_ref[:, lanes] = lax.bitcast_convert_type(w << 16, F32).astype(BF16)
        o_ref[:, half + c * V7X_LANES:half + (c + 1) * V7X_LANES] = (
            lax.bitcast_convert_type(w & jnp.uint32(0xFFFF0000), F32).astype(BF16))


def _gather_rows(x, n_tok, tok, tm):
    r_total = tok.shape[0]
    wr = x.shape[0] // n_tok
    d = 2 * wr * V7X_LANES
    return pl.pallas_call(
        functools.partial(_gather_body, tm=tm, wr=wr),
        grid_spec=pltpu.PrefetchScalarGridSpec(
            num_scalar_prefetch=1,
            grid=(r_total // tm,),
            in_specs=[pl.BlockSpec(memory_space=pl.ANY)],
            out_specs=pl.BlockSpec((tm, d), lambda i, tok: (i, 0)),
            scratch_shapes=[pltpu.VMEM((2, tm * wr, V7X_LANES), x.dtype),
                            pltpu.SemaphoreType.DMA((2,))]),
        out_shape=jax.ShapeDtypeStruct((r_total, d), BF16),
        compiler_params=_cparams(("arbitrary",)),
        name="moe_gather",
    )(tok, x)


def _swiglu_chunk(h, sel):
    gate = jnp.minimum(h, SWIGLU_LIMIT)
    up = jnp.clip(h, -SWIGLU_LIMIT, SWIGLU_LIMIT)
    glu = gate * _sigmoid(SWIGLU_ALPHA * gate)
    halves = []
    for c in range(2):
        sl = slice(c * V7X_LANES, (c + 1) * V7X_LANES)
        up_next = pltpu.roll(up[:, sl], V7X_LANES - 1, axis=1)
        halves.append(((up_next + 1.0) * glu[:, sl]).astype(BF16))
    prod = jnp.concatenate(halves, axis=1)
    return jnp.dot(prod, sel, preferred_element_type=F32).astype(BF16)


def _expert_mm_body(bstart_ref, nblk_ref, x_hbm, w_ref, b_ref, *rest, tm, swiglu, n_blocks):
    if swiglu:
        sel_ref, o_hbm, wbf_ref, xbuf, obuf, xsem, osem = rest
    else:
        o_hbm, wbf_ref, xbuf, obuf, xsem, osem = rest
    n = pl.program_id(0)
    e = pl.program_id(1)
    nb = nblk_ref[e]
    b0 = bstart_ref[e]
    otn = obuf.shape[2]
    col0 = pl.multiple_of(n * otn, otn)
    chunk = 2 * V7X_LANES
    n_chunks = wbf_ref.shape[1] // chunk
    gb = GROUP_BLOCKS
    n_full = nb // gb
    rem = nb % gb
    pieces = [p for p in (8, 4, 2, 1) if p < gb]
    assert gb <= 2 * pieces[0]

    def x_block_copy(blk, slot, q):
        r = pl.multiple_of((b0 + blk) * tm, tm)
        return pltpu.make_async_copy(x_hbm.at[pl.ds(r, tm), :],
                                     xbuf.at[slot, pl.ds(q * tm, tm), :], xsem.at[slot])

    def x_wait(slot, blocks):
        pltpu.make_async_copy(x_hbm.at[pl.ds(0, blocks * tm), :],
                              xbuf.at[slot, pl.ds(0, blocks * tm), :], xsem.at[slot]).wait()

    def o_copy(blk, slot, q, blocks):
        r = pl.multiple_of((b0 + blk) * tm, tm)
        return pltpu.make_async_copy(obuf.at[slot, pl.ds(q * tm, blocks * tm), :],
                                     o_hbm.at[pl.ds(r, blocks * tm), pl.ds(col0, otn)], osem.at[slot])

    def fetch_group(g, slot):
        for q in range(gb):
            @pl.when(g * gb + q < nb)
            def _():
                x_block_copy(g * gb + q, slot, q).start()

    def compute(slot, q, blocks):
        rows = slice(q * tm, (q + blocks) * tm)
        x = xbuf[slot, rows, :]

        def column_chunk(c):
            cs = slice(c * chunk, (c + 1) * chunk)
            return jnp.dot(x, wbf_ref[:, cs], preferred_element_type=F32) + b_ref[:, cs]

        h = column_chunk(0)
        for c in range(n_chunks):
            h_next = column_chunk(c + 1) if c + 1 < n_chunks else None
            if swiglu:
                obuf[slot, rows, c * V7X_LANES:(c + 1) * V7X_LANES] = _swiglu_chunk(h, sel_ref[...])
            else:
                obuf[slot, rows, c * chunk:(c + 1) * chunk] = h
            h = h_next

    @pl.when(nb > 0)
    def _():
        fetch_group(0, 0)
        _cast_weight(w_ref, wbf_ref)

    def full_group(g, carry):
        slot = g % 2
        x_wait(slot, gb)
        fetch_group(g + 1, 1 - slot)

        @pl.when(g >= 2)
        def _():
            o_copy(0, slot, 0, gb).wait()

        compute(slot, 0, gb)
        o_copy(g * gb, slot, 0, gb).start()
        return carry

    lax.fori_loop(0, n_full, full_group, 0)

    last_slot = n_full % 2

    @pl.when((rem > 0) & (n_full >= 2))
    def _():
        o_copy(0, last_slot, 0, gb).wait()

    q = 0
    piece_pos = []
    positions = [0]
    for p in pieces:
        has = (rem & p) != 0
        piece_pos.append((p, q, has, sorted(s for s in set(positions) if s + p < gb)))
        q = q + jnp.where(has, p, 0)
        positions = positions + [s + p for s in positions]
    for p, qpos, has, _ in piece_pos:
        @pl.when(has)
        def _(p=p):
            x_wait(last_slot, p)
    for p, qpos, has, q_options in piece_pos:
        for q_static in q_options:
            @pl.when(has & (qpos == q_static))
            def _(p=p, q_static=q_static):
                compute(last_slot, q_static, p)
                o_copy(n_full * gb + q_static, last_slot, q_static, p).start()

    @pl.when((rem == 0) & (n_full >= 2))
    def _():
        o_copy(0, last_slot, 0, gb).wait()

    @pl.when(n_full >= 1)
    def _():
        o_copy(0, 1 - last_slot, 0, gb).wait()

    for p, qpos, has, _ in piece_pos:
        @pl.when(has)
        def _(p=p):
            o_copy(0, last_slot, 0, p).wait()

    @pl.when(e == pl.num_programs(1) - 1)
    def _():
        obuf[0, 0:tm, :] = jnp.zeros((tm, otn), obuf.dtype)

        def zstart(blk, carry):
            o_copy(blk, 0, 0, 1).start()
            return carry

        def zwait(blk, carry):
            o_copy(blk, 0, 0, 1).wait()
            return carry

        lax.fori_loop(nb, n_blocks - b0, zstart, 0)
        lax.fori_loop(nb, n_blocks - b0, zwait, 0)


def _expert_mm(x, w, bias, bstart, nblk, tm, tn, swiglu):
    r_total, k = x.shape
    n_e, _, n_total = w.shape
    out_tn = tn // 2 if swiglu else tn
    out_n = n_total // 2 if swiglu else n_total
    out_dtype = BF16 if swiglu else F32
    in_specs = [pl.BlockSpec(memory_space=pl.ANY),
                pl.BlockSpec((None, k, tn), lambda n, e, bs, nb: (e, 0, n)),
                pl.BlockSpec((None, 1, tn), lambda n, e, bs, nb: (e, 0, n))]
    args = [x, w, bias.reshape(n_e, 1, n_total)]
    if swiglu:
        sel = (lax.broadcasted_iota(jnp.int32, (2 * V7X_LANES, V7X_LANES), 0)
               == 2 * lax.broadcasted_iota(jnp.int32, (2 * V7X_LANES, V7X_LANES), 1)).astype(BF16)
        in_specs.append(pl.BlockSpec((2 * V7X_LANES, V7X_LANES), lambda n, e, bs, nb: (0, 0)))
        args.append(sel)
    return pl.pallas_call(
        functools.partial(_expert_mm_body, tm=tm, swiglu=swiglu, n_blocks=r_total // tm),
        grid_spec=pltpu.PrefetchScalarGridSpec(
            num_scalar_prefetch=2,
            grid=(n_total // tn, n_e),
            in_specs=in_specs,
            out_specs=pl.BlockSpec(memory_space=pl.ANY),
            scratch_shapes=[pltpu.VMEM((k, tn), BF16),
                            pltpu.VMEM((2, GROUP_BLOCKS * tm, k), BF16),
                            pltpu.VMEM((2, GROUP_BLOCKS * tm, out_tn), out_dtype),
                            pltpu.SemaphoreType.DMA((2,)),
                            pltpu.SemaphoreType.DMA((2,))]),
        out_shape=jax.ShapeDtypeStruct((r_total, out_n), out_dtype),
        compiler_params=_cparams(("arbitrary", "arbitrary")),
        name="moe_gate_up" if swiglu else "moe_down",
    )(bstart, nblk, *args)


def _combine_rows_copy(src_hbm, buf_ref, sem_ref, row, slot, k, r):
    return pltpu.make_async_copy(src_hbm.at[pl.ds(row, 1), :],
                                 buf_ref.at[slot, k, pl.ds(r, 1), :],
                                 sem_ref.at[slot])


def _combine_body(dest_ref, yb_hbm, x_ref, gate_ref, g_ref, o_ref, buf_ref, sem_ref, *, tm):
    i = pl.program_id(0)
    n = pl.num_programs(0)

    def issue(block, slot):
        def body(r, carry):
            for k in range(TOP_K):
                row = dest_ref[(block * tm + r) * TOP_K + k]
                _combine_rows_copy(yb_hbm, buf_ref, sem_ref, row, slot, k, r).start()
            return carry
        lax.fori_loop(0, tm, body, 0, unroll=2)

    @pl.when(i == 0)
    def _():
        issue(0, 0)

    @pl.when(i + 1 < n)
    def _():
        issue(i + 1, (i + 1) % 2)

    slot = i % 2
    for k in range(TOP_K):
        pltpu.make_async_copy(yb_hbm.at[pl.ds(0, tm), :], buf_ref.at[slot, k], sem_ref.at[slot]).wait()
    gates = gate_ref[...]
    acc = x_ref[...]
    for k in range(TOP_K):
        acc = acc + gates[:, k:k + 1] * buf_ref[slot, k]
    ms = jnp.mean(acc * acc, axis=-1, keepdims=True)
    o_ref[...] = acc * lax.rsqrt(ms + RMS_EPS) * g_ref[...]


def _combine(yb, dest_flat, x, gates, g, tm):
    t, d = x.shape
    return pl.pallas_call(
        functools.partial(_combine_body, tm=tm),
        grid_spec=pltpu.PrefetchScalarGridSpec(
            num_scalar_prefetch=1,
            grid=(t // tm,),
            in_specs=[pl.BlockSpec(memory_space=pl.ANY),
                      pl.BlockSpec((tm, d), lambda i, dest: (i, 0)),
                      pl.BlockSpec((tm, V7X_LANES), lambda i, dest: (i, 0)),
                      pl.BlockSpec((1, d), lambda i, dest: (0, 0))],
            out_specs=pl.BlockSpec((tm, d), lambda i, dest: (i, 0)),
            scratch_shapes=[pltpu.VMEM((2, TOP_K, tm, d), F32), pltpu.SemaphoreType.DMA((2,))]),
        out_shape=jax.ShapeDtypeStruct((t, d), F32),
        compiler_params=_cparams(("arbitrary",)),
        name="moe_combine",
    )(dest_flat, yb, x, gates, g.reshape(1, d))


def _routing_tables(top_idx, n_e, tm):
    t = top_idx.shape[0]
    p = t * TOP_K
    flat_e = top_idx.reshape(p)
    onehot = (flat_e[:, None] == jnp.arange(n_e, dtype=jnp.int32)[None, :]).astype(jnp.int32)
    rank = jnp.take_along_axis(jnp.cumsum(onehot, axis=0), flat_e[:, None], axis=1)[:, 0] - 1
    counts = jnp.sum(onehot, axis=0)
    pcounts = (counts + tm - 1) // tm * tm
    pends = jnp.cumsum(pcounts)
    pstarts = pends - pcounts
    dest = (pstarts[flat_e] + rank).astype(jnp.int32)
    n_blocks = p // tm + n_e
    flat_tok = jnp.arange(p, dtype=jnp.int32) // TOP_K
    slot_tok = jnp.zeros((n_blocks * tm,), jnp.int32).at[dest].set(flat_tok)
    return dest, slot_tok, (pstarts // tm).astype(jnp.int32), (pcounts // tm).astype(jnp.int32)


def kernel(x_prompt, x_sample, state_s5_re, state_s5_im, state_hgrn, norm_mix, w_in, b_gate, s5_lam_re, s5_lam_im, s5_log_step, s5_b_re, s5_b_im, s5_c_re, s5_c_im, s5_d, s5_w_glu, hgrn_lb_logits, hgrn_norm, w_branch_s5, w_branch_hgrn, w_out, norm_ffn, w_router, b_router, w_gate_up, b_gate_up, w_down, b_down, norm_final):
    depth = w_in.shape[0]
    assert depth == 1, "single-layer stack"
    n_p, s_p, d = x_prompt.shape
    n_s, s_s, _ = x_sample.shape
    t_p, t_s = n_p * s_p, n_s * s_s
    g_s5, p_s5 = s5_lam_re.shape[1:]
    d_s5 = g_s5 * S5_GROUP
    n_heads = state_hgrn.shape[2]
    d_h = n_heads * HGRN_HEAD_DIM
    n_e = w_router.shape[2]
    l = 0

    lb_all = jnp.cumsum(jax.nn.softmax(hgrn_lb_logits.astype(F32), axis=0), axis=0)[:depth]
    x = jnp.concatenate([x_prompt.reshape(t_p, d), x_sample.reshape(t_s, d)], axis=0)
    t = t_p + t_s

    h = _rmsnorm(x, norm_mix[l], BF16)
    z = _dense_mm([(h, 0)], [w_in[l]], [], lambda accs, ex: accs[0], F32, tm=1024, tn=512, name="in_proj")

    wb, wc, lam = _s5_params(s5_lam_re[l], s5_lam_im[l], s5_log_step[l], s5_b_re[l], s5_b_im[l],
                             s5_c_re[l], s5_c_im[l])
    d_flat = s5_d[l].reshape(1, d_s5)
    zeros_s5 = jnp.zeros((n_p, g_s5 * p_s5), F32)
    y_p, p_re, p_im = _s5_scan(z, 0, n_p, s_p, zeros_s5, zeros_s5, wb, wc, lam, d_flat, part_rows=256)
    y_s, s_re, s_im = _s5_scan(z, t_p, n_s, s_s, state_s5_re[l].reshape(n_s, -1),
                               state_s5_im[l].reshape(n_s, -1), wb, wc, lam, d_flat, part_rows=256)
    y = jnp.concatenate([y_p, y_s], axis=0)
    a = _dense_mm([(y, 0)], [s5_w_glu[l]],
                  [(y, (1024, 512), lambda n, m: (m, n))],
                  lambda accs, ex: ex[0] * _sigmoid(accs[0]), BF16, tm=1024, tn=512, name="s5_glu")

    lb_flat = lb_all[l].reshape(1, d_h)
    cb0 = d_s5 // d_h
    zeros_hg = jnp.zeros((n_p,) + state_hgrn.shape[2:], F32)
    o_p, p_hg = _hgrn(z, 0, n_p, s_p, zeros_hg, lb_flat, hgrn_norm[l], cb0)
    o_s, s_hg = _hgrn(z, t_p, n_s, s_s, state_hgrn[l], lb_flat, hgrn_norm[l], cb0)
    b = jnp.concatenate([o_p, o_s], axis=0)

    tn = 512
    ga_cb = (d_s5 + 4 * d_h) // tn
    gb_cb = ga_cb + d // tn
    bg = b_gate[l].reshape(1, 2 * d)
    merged = _dense_mm(
        [(a, 0), (b, 0)], [w_branch_s5[l], w_branch_hgrn[l]],
        [(z, (512, tn), lambda n, m: (m, ga_cb + n)),
         (z, (512, tn), lambda n, m: (m, gb_cb + n)),
         (bg, (1, tn), lambda n, m: (0, n)),
         (bg, (1, tn), lambda n, m: (0, d // tn + n))],
        lambda accs, ex: (_sigmoid(ex[0] + ex[2]) * accs[0] + _sigmoid(ex[1] + ex[3]) * accs[1]),
        BF16, tm=512, tn=tn, name="merge")
    x1 = _dense_mm([(merged, 0)], [w_out[l]], [(x, (1024, 512), lambda n, m: (m, n))],
                   lambda accs, ex: ex[0] + accs[0], F32, tm=1024, tn=512, name="out_proj")

    tm_e = 256
    xn, idx_wide, gate_wide = _router(x1, norm_ffn[l], w_router[l], b_router[l])
    top_idx = idx_wide[:, :TOP_K]
    dest, slot_tok, bstart, nblk = _routing_tables(top_idx, n_e, tm_e)
    xs = _gather_rows(xn, t, slot_tok, tm_e)
    act = _expert_mm(xs, w_gate_up[l], b_gate_up[l], bstart, nblk, tm_e, 512, swiglu=True)
    yb = _expert_mm(act, w_down[l], b_down[l], bstart, nblk, tm_e, 512, swiglu=False)
    y_out = _combine(yb, dest, x1, gate_wide, norm_final, tm=128)

    y_prompt = y_out[:t_p].reshape(n_p, s_p, d)
    y_sample = y_out[t_p:].reshape(n_s, s_s, d)
    return (y_prompt, y_sample,
            p_re.reshape(1, n_p, g_s5, p_s5), p_im.reshape(1, n_p, g_s5, p_s5), p_hg[None],
            s_re.reshape(1, n_s, g_s5, p_s5), s_im.reshape(1, n_s, g_s5, p_s5), s_hg[None])
```

```python
import functools

import jax
import jax.numpy as jnp
from jax import lax
from jax.experimental import pallas as pl
from jax.experimental.pallas import tpu as pltpu

F32 = jnp.float32
BF16 = jnp.bfloat16

RMS_EPS = 1e-5
S5_GROUP = 16
S5_STATE = 64
S5_GROUPS_PER_BLOCK = 16
HGRN_HEAD_DIM = 128
HGRN_CHUNK = 32
TOP_K = 4
SWIGLU_LIMIT = 7.0
SWIGLU_ALPHA = 1.702
EXP_CLAMP = 80.0

V7X_LANES = 128
V7X_SUBLANES = 8
V7X_BF16_SUBLANES = 16
V7X_VMEM_LIMIT = 56 * 1024 * 1024
CAST_ROWS = 256
SUPER_BLOCKS = 8
ISSUE_UNROLL = 8


def _cparams(semantics, vmem=V7X_VMEM_LIMIT):
    return pltpu.CompilerParams(dimension_semantics=semantics, vmem_limit_bytes=vmem)


def _sigmoid(x):
    return 1.0 / (1.0 + jnp.exp(-x))


def _cast_weight(w_ref, wbf_ref, row0=0):
    k = w_ref.shape[0]
    rows = min(CAST_ROWS, k)

    def body(i, carry):
        r = pl.multiple_of(i * rows, rows)
        wbf_ref[pl.ds(row0 + r, rows), :] = w_ref[pl.ds(r, rows), :].astype(BF16)
        return carry

    lax.fori_loop(0, k // rows, body, 0)


def _rmsnorm_body(x_ref, g_ref, o_ref):
    x = x_ref[...]
    ms = jnp.mean(x * x, axis=-1, keepdims=True)
    o_ref[...] = (x * lax.rsqrt(ms + RMS_EPS) * g_ref[...]).astype(o_ref.dtype)


def _rmsnorm(x, g, out_dtype, tm=256):
    t, d = x.shape
    return pl.pallas_call(
        _rmsnorm_body,
        grid=(t // tm,),
        in_specs=[pl.BlockSpec((tm, d), lambda i: (i, 0)),
                  pl.BlockSpec((1, d), lambda i: (0, 0))],
        out_specs=pl.BlockSpec((tm, d), lambda i: (i, 0)),
        out_shape=jax.ShapeDtypeStruct((t, d), out_dtype),
        compiler_params=_cparams(("arbitrary",)),
        name="rmsnorm",
    )(x, g.reshape(1, d))


def _mm_body(*refs, n_x, n_extra, epilogue):
    x_refs = refs[:n_x]
    w_refs = refs[n_x:2 * n_x]
    extra_refs = refs[2 * n_x:2 * n_x + n_extra]
    o_ref = refs[2 * n_x + n_extra]
    wbf_refs = refs[2 * n_x + n_extra + 1:]

    @pl.when(pl.program_id(1) == 0)
    def _():
        for w_ref, wbf_ref in zip(w_refs, wbf_refs):
            _cast_weight(w_ref, wbf_ref)

    accs = [jnp.dot(x_ref[...].astype(BF16), wbf_ref[...], preferred_element_type=F32)
            for x_ref, wbf_ref in zip(x_refs, wbf_refs)]
    o_ref[...] = epilogue(accs, [e[...] for e in extra_refs]).astype(o_ref.dtype)


def _dense_mm(xs, ws, extras, epilogue, out_dtype, tm, tn, name):
    m_total = xs[0][0].shape[0]
    n_total = ws[0].shape[1]
    in_specs = []
    for (x, cb), w in zip(xs, ws):
        in_specs.append(pl.BlockSpec((tm, w.shape[0]), functools.partial(lambda n, m, cb: (m, cb), cb=cb)))
    for w in ws:
        in_specs.append(pl.BlockSpec((w.shape[0], tn), lambda n, m: (0, n)))
    for _, bshape, imap in extras:
        in_specs.append(pl.BlockSpec(bshape, imap))
    body = functools.partial(_mm_body, n_x=len(xs), n_extra=len(extras), epilogue=epilogue)
    return pl.pallas_call(
        body,
        grid=(n_total // tn, m_total // tm),
        in_specs=in_specs,
        out_specs=pl.BlockSpec((tm, tn), lambda n, m: (m, n)),
        out_shape=jax.ShapeDtypeStruct((m_total, n_total), out_dtype),
        scratch_shapes=[pltpu.VMEM((w.shape[0], tn), BF16) for w in ws],
        compiler_params=_cparams(("arbitrary", "arbitrary")),
        name=name,
    )(*[x for x, _ in xs], *ws, *[e for e, _, _ in extras])


def _s5_params(lam_re, lam_im, log_step, b_re, b_im, c_re, c_im):
    g, p = lam_re.shape
    h = b_re.shape[-1]
    gb = S5_GROUPS_PER_BLOCK
    nb = g // gb
    dt = jnp.exp(log_step)[:, None]
    mag = jnp.exp(lam_re * dt)
    ang = lam_im * dt
    lbar_re, lbar_im = mag * jnp.cos(ang), mag * jnp.sin(ang)
    nr, ni = lbar_re - 1.0, lbar_im
    den = lam_re * lam_re + lam_im * lam_im
    fr = (nr * lam_re + ni * lam_im) / den
    fi = (ni * lam_re - nr * lam_im) / den
    bb_re = fr[:, :, None] * b_re - fi[:, :, None] * b_im
    bb_im = fr[:, :, None] * b_im + fi[:, :, None] * b_re
    eye = jnp.eye(gb, dtype=F32)

    def blockdiag_in(bb):
        t = bb.reshape(nb, gb, p, h)
        return jnp.einsum('jgph,gk->jghkp', t, eye).reshape(nb, gb * h, gb * p)

    def blockdiag_out(c):
        t = c.reshape(nb, gb, h, p)
        return jnp.einsum('jghp,gk->jgpkh', t, eye).reshape(nb, gb * p, gb * h)

    wb = jnp.concatenate([blockdiag_in(bb_re), blockdiag_in(bb_im)], axis=2).astype(BF16)
    wc = jnp.concatenate([blockdiag_out(c_re), -blockdiag_out(c_im)], axis=1).astype(BF16)
    lam = jnp.concatenate([lbar_re.reshape(nb, 1, gb * p), lbar_im.reshape(nb, 1, gb * p)], axis=2)
    return wb, wc, lam


def _s5_body(*refs, n_parts, nsb, seq_rows, spb, part_pitch, seq_pitch):
    u_refs = refs[:n_parts]
    (sre_ref, sim_ref, wb_ref, wc_ref, lam_ref, d_ref,
     y_ref, nre_ref, nim_ref, bu_ref, xs_ref, cre_ref, cim_ref) = refs[n_parts:]
    hs = cre_ref.shape[1]
    pr = u_refs[0].shape[0]
    nc = hs // V7X_LANES
    for p, u_ref in enumerate(u_refs):
        bu = jnp.dot(u_ref[...].astype(BF16), wb_ref[...], preferred_element_type=F32)
        for c in range(2 * nc):
            bu_ref[c, p * part_pitch:p * part_pitch + pr, :] = bu[:, c * V7X_LANES:(c + 1) * V7X_LANES]

    @pl.when(pl.program_id(1) % spb == 0)
    def _():
        cre_ref[...] = sre_ref[...]
        cim_ref[...] = sim_ref[...]

    def lanes(c):
        return slice(c * V7X_LANES, (c + 1) * V7X_LANES)

    for g0 in range(0, nsb, V7X_SUBLANES):
        ns = min(V7X_SUBLANES, nsb - g0)
        lr = [jnp.broadcast_to(lam_ref[:, lanes(c)], (ns, V7X_LANES)) for c in range(nc)]
        li = [jnp.broadcast_to(lam_ref[:, lanes(nc + c)], (ns, V7X_LANES)) for c in range(nc)]

        def t_body(t, carry, g0=g0, ns=ns, lr=lr, li=li):
            xr, xi = carry
            rows = pl.ds(g0 * seq_pitch + t, ns, stride=seq_pitch)
            nr, ni = [], []
            for c in range(nc):
                r = lr[c] * xr[c] - li[c] * xi[c] + bu_ref[c, rows, :]
                i = lr[c] * xi[c] + li[c] * xr[c] + bu_ref[nc + c, rows, :]
                xs_ref[c, rows, :] = r
                xs_ref[nc + c, rows, :] = i
                nr.append(r)
                ni.append(i)
            return tuple(nr), tuple(ni)

        init = (tuple(cre_ref[g0:g0 + ns, lanes(c)] for c in range(nc)),
                tuple(cim_ref[g0:g0 + ns, lanes(c)] for c in range(nc)))
        xr, xi = lax.fori_loop(0, seq_rows, t_body, init, unroll=8)
        for c in range(nc):
            cre_ref[g0:g0 + ns, lanes(c)] = xr[c]
            cim_ref[g0:g0 + ns, lanes(c)] = xi[c]

    for p, u_ref in enumerate(u_refs):
        xs = jnp.concatenate([xs_ref[c, p * part_pitch:p * part_pitch + pr, :].astype(BF16)
                              for c in range(2 * nc)], axis=1)
        y = jnp.dot(xs, wc_ref[...], preferred_element_type=F32) + d_ref[...] * u_ref[...]
        y_ref[p] = jax.nn.gelu(y)
    nre_ref[...] = cre_ref[...]
    nim_ref[...] = cim_ref[...]


def _s5_scan(z, row0, n_seq, seq_len, st_re, st_im, wb, wc, lam, d_flat, part_rows):
    nb, bw, sw = wb.shape
    hs = sw // 2
    if seq_len >= part_rows:
        n_parts, nsb, seq_rows = n_seq, n_seq, part_rows
        n_rb = seq_len // part_rows
        spb = n_rb
    else:
        n_parts, nsb, seq_rows = 1, part_rows // seq_len, seq_len
        n_rb = n_seq * seq_len // part_rows
        spb = 1
    n_sblk = n_rb // spb
    rb0 = row0 // part_rows
    part_stride = seq_len // part_rows if n_parts > 1 else 0
    st_re3 = st_re.reshape(n_sblk, nsb, nb * hs)
    st_im3 = st_im.reshape(n_sblk, nsb, nb * hs)
    part_pitch = part_rows + V7X_SUBLANES if n_parts > 1 else part_rows
    seq_pitch = part_pitch if n_parts > 1 else seq_rows
    body = functools.partial(_s5_body, n_parts=n_parts, nsb=nsb, seq_rows=seq_rows, spb=spb,
                             part_pitch=part_pitch, seq_pitch=seq_pitch)
    state_spec = pl.BlockSpec((None, nsb, hs), lambda j, r: (r // spb, 0, j))
    u_specs = [pl.BlockSpec((part_rows, bw),
                            functools.partial(lambda j, r, p: (rb0 + p * part_stride + r, j), p=p))
               for p in range(n_parts)]
    rows_per_part = n_seq * seq_len // n_parts
    y, nre, nim = pl.pallas_call(
        body,
        grid=(nb, n_rb),
        in_specs=u_specs + [
            state_spec, state_spec,
            pl.BlockSpec((None, bw, sw), lambda j, r: (j, 0, 0)),
            pl.BlockSpec((None, sw, bw), lambda j, r: (j, 0, 0)),
            pl.BlockSpec((None, 1, sw), lambda j, r: (j, 0, 0)),
            pl.BlockSpec((1, bw), lambda j, r: (0, j))],
        out_specs=[pl.BlockSpec((n_parts, part_rows, bw), lambda j, r: (0, r, j)),
                   state_spec, state_spec],
        out_shape=[jax.ShapeDtypeStruct((n_parts, rows_per_part, nb * bw), F32),
                   jax.ShapeDtypeStruct((n_sblk, nsb, nb * hs), F32),
                   jax.ShapeDtypeStruct((n_sblk, nsb, nb * hs), F32)],
        scratch_shapes=[pltpu.VMEM((sw // V7X_LANES, n_parts * part_pitch, V7X_LANES), F32),
                        pltpu.VMEM((sw // V7X_LANES, n_parts * part_pitch, V7X_LANES), F32),
                        pltpu.VMEM((nsb, hs), F32),
                        pltpu.VMEM((nsb, hs), F32)],
        compiler_params=_cparams(("arbitrary", "arbitrary")),
        name="s5_scan",
    )(*([z] * n_parts), st_re3, st_im3, wb, wc, lam, d_flat)
    return (y.reshape(n_seq * seq_len, nb * bw),
            nre.reshape(n_seq, nb * hs), nim.reshape(n_seq, nb * hs))


def _cumsum_rows(x, seg):
    row = lax.broadcasted_iota(jnp.int32, x.shape, 0) % seg
    s = 1
    while s < seg:
        x = x + jnp.where(row >= s, pltpu.roll(x, s, axis=0), 0.0)
        s *= 2
    return x


def _hgrn_body(q_ref, f_ref, v_ref, g_ref, st_ref, lb_ref, ng_ref, o_ref, ns_ref, s_ref,
               *, rows, prows, n_chunks, n_heads):
    kd = HGRN_HEAD_DIM
    c = pl.program_id(1)

    @pl.when(c == 0)
    def _():
        s_ref[...] = st_ref[...]

    lb = lb_ref[...]
    fz = f_ref[...]
    q = q_ref[...]
    log_f = jnp.log(lb + (1.0 - lb) * _sigmoid(fz))
    kk = (1.0 - lb) * _sigmoid(-fz)
    qh = q * _sigmoid(q) * (kd ** -0.5)
    v = v_ref[...]
    if prows > rows:
        pad = jnp.zeros((prows - rows, fz.shape[1]), F32)
        log_f, kk, qh, v = [jnp.concatenate([a, pad], axis=0) for a in (log_f, kk, qh, v)]
    b = _cumsum_rows(log_f, prows)
    b_end = b[rows - 1:rows, :]
    b_mid = b[rows // 2 - 1:rows // 2, :]
    qt = (qh * jnp.exp(jnp.minimum(b - b_mid, EXP_CLAMP))).astype(BF16)
    kt = (kk * jnp.exp(jnp.minimum(b_mid - b, EXP_CLAMP))).astype(BF16)
    qe = (qh * jnp.exp(b)).astype(BF16)
    ke = (kk * jnp.exp(b_end - b)).astype(BF16)
    dec = jnp.exp(b_end)
    vb = v.astype(BF16)
    causal = (lax.broadcasted_iota(jnp.int32, (prows, prows), 0)
              >= lax.broadcasted_iota(jnp.int32, (prows, prows), 1))
    eye = (lax.broadcasted_iota(jnp.int32, (kd, kd), 0)
           == lax.broadcasted_iota(jnp.int32, (kd, kd), 1))
    og = g_ref[...]
    gate = og * _sigmoid(og)
    ng = ng_ref[...]
    for h in range(n_heads):
        sl = slice(h * kd, (h + 1) * kd)
        sc = lax.dot_general(qt[:, sl], kt[:, sl], (((1,), (1,)), ((), ())),
                             preferred_element_type=F32)
        sc = jnp.where(causal, sc, 0.0).astype(BF16)
        s_prev = s_ref[h]
        o = (jnp.dot(sc, vb[:, sl], preferred_element_type=F32)
             + jnp.dot(qe[:, sl], s_prev.astype(BF16), preferred_element_type=F32))
        upd = lax.dot_general(ke[:, sl], vb[:, sl], (((0,), (0,)), ((), ())),
                              preferred_element_type=F32)
        dec_col = jnp.sum(jnp.where(eye, jnp.broadcast_to(dec[:, sl], (kd, kd)), 0.0),
                          axis=1, keepdims=True)
        s_ref[h] = dec_col * s_prev + upd
        o = o[:rows]
        o = o * lax.rsqrt(jnp.mean(o * o, axis=-1, keepdims=True) + RMS_EPS) * ng
        o_ref[:, sl] = o * gate[:, sl]

    @pl.when(c == n_chunks - 1)
    def _():
        ns_ref[...] = s_ref[...]


def _hgrn(z, row0, n_seq, seq_len, state, lb_flat, norm_g, col_block0):
    n_heads, kd = state.shape[1], state.shape[2]
    dh = n_heads * kd
    rows = min(HGRN_CHUNK, seq_len)
    prows = max(rows, V7X_BF16_SUBLANES)
    n_chunks = seq_len // rows
    rb0 = row0 // rows
    body = functools.partial(_hgrn_body, rows=rows, prows=prows, n_chunks=n_chunks, n_heads=n_heads)

    def zspec(i):
        return pl.BlockSpec((rows, dh), lambda s, c: (rb0 + s * n_chunks + c, col_block0 + i))

    st_spec = pl.BlockSpec((None, n_heads, kd, kd), lambda s, c: (s, 0, 0, 0))
    o, ns = pl.pallas_call(
        body,
        grid=(n_seq, n_chunks),
        in_specs=[zspec(0), zspec(1), zspec(2), zspec(3), st_spec,
                  pl.BlockSpec((1, dh), lambda s, c: (0, 0)),
                  pl.BlockSpec((1, kd), lambda s, c: (0, 0))],
        out_specs=[pl.BlockSpec((rows, dh), lambda s, c: (s * n_chunks + c, 0)), st_spec],
        out_shape=[jax.ShapeDtypeStruct((n_seq * seq_len, dh), F32),
                   jax.ShapeDtypeStruct(state.shape, F32)],
        scratch_shapes=[pltpu.VMEM((n_heads, kd, kd), F32)],
        compiler_params=_cparams(("arbitrary", "arbitrary")),
        name="hgrn",
    )(z, z, z, z, state, lb_flat, norm_g.reshape(1, kd))
    return o, ns


def _router_body(x_ref, g_ref, w_ref, b_ref, xn_ref, idx_ref, gate_ref):
    x = x_ref[...]
    ms = jnp.mean(x * x, axis=-1, keepdims=True)
    xn = x * lax.rsqrt(ms + RMS_EPS) * g_ref[...]
    tm, d = xn.shape
    half = d // 2
    lo = lax.bitcast_convert_type(xn[:, :half].astype(BF16).astype(F32), jnp.uint32)
    hi = lax.bitcast_convert_type(xn[:, half:].astype(BF16).astype(F32), jnp.uint32)
    packed = (lo >> 16) | (hi & jnp.uint32(0xFFFF0000))
    wr = half // V7X_LANES
    for c in range(wr):
        xn_ref[pl.ds(c, tm, stride=wr), :] = packed[:, c * V7X_LANES:(c + 1) * V7X_LANES]
    logits = jnp.dot(xn.astype(BF16), w_ref[...].astype(BF16), preferred_element_type=F32) + b_ref[...]
    n_e = logits.shape[1]
    lane = lax.broadcasted_iota(jnp.int32, logits.shape, 1)
    out_lane = lax.broadcasted_iota(jnp.int32, idx_ref.shape, 1)
    idx_out = jnp.zeros(idx_ref.shape, jnp.int32)
    val_out = jnp.full(gate_ref.shape, -jnp.inf, F32)
    work = logits
    for k in range(TOP_K):
        m = jnp.max(work, axis=-1, keepdims=True)
        i = jnp.min(jnp.where(work == m, lane, n_e), axis=-1, keepdims=True)
        idx_out = jnp.where(out_lane == k, i, idx_out)
        val_out = jnp.where(out_lane == k, m, val_out)
        work = jnp.where(lane == i, -jnp.inf, work)
    top = jnp.max(val_out, axis=-1, keepdims=True)
    e = jnp.exp(val_out - top)
    gate_ref[...] = e / jnp.sum(e, axis=-1, keepdims=True)
    idx_ref[...] = idx_out


def _router(x, g, w_router, b_router, tm=256):
    t, d = x.shape
    n_e = w_router.shape[1]
    row = lambda i: (i, 0)
    fixed = lambda i: (0, 0)
    return pl.pallas_call(
        _router_body,
        grid=(t // tm,),
        in_specs=[pl.BlockSpec((tm, d), row), pl.BlockSpec((1, d), fixed),
                  pl.BlockSpec((d, n_e), fixed), pl.BlockSpec((1, n_e), fixed)],
        out_specs=[pl.BlockSpec((tm * (d // 2 // V7X_LANES), V7X_LANES), row),
                   pl.BlockSpec((tm, V7X_LANES), row), pl.BlockSpec((tm, V7X_LANES), row)],
        out_shape=[jax.ShapeDtypeStruct((t * (d // 2 // V7X_LANES), V7X_LANES), jnp.uint32),
                   jax.ShapeDtypeStruct((t, V7X_LANES), jnp.int32),
                   jax.ShapeDtypeStruct((t, V7X_LANES), F32)],
        compiler_params=_cparams(("arbitrary",)),
        name="router",
    )(x, g.reshape(1, d), w_router, b_router.reshape(1, n_e))


def _gather_rows_copy(src_hbm, buf_ref, sem_ref, tok, slot, r, wr):
    return pltpu.make_async_copy(src_hbm.at[pl.ds(pl.multiple_of(tok * wr, wr), wr), :],
                                 buf_ref.at[slot, pl.ds(pl.multiple_of(r * wr, wr), wr), :],
                                 sem_ref.at[slot])


def _gather_body(tok_ref, x_hbm, o_ref, buf_ref, sem_ref, *, tm, wr):
    i = pl.program_id(0)
    n = pl.num_programs(0)

    def issue(block, slot):
        def body(r, carry):
            _gather_rows_copy(x_hbm, buf_ref, sem_ref, tok_ref[block * tm + r], slot, r, wr).start()
            return carry
        lax.fori_loop(0, tm, body, 0, unroll=ISSUE_UNROLL)

    @pl.when(i == 0)
    def _():
        issue(0, 0)

    @pl.when(i + 1 < n)
    def _():
        issue(i + 1, (i + 1) % 2)

    slot = i % 2
    pltpu.make_async_copy(x_hbm.at[pl.ds(0, tm * wr), :], buf_ref.at[slot], sem_ref.at[slot]).wait()
    half = wr * V7X_LANES
    for c in range(wr):
        w = buf_ref[slot, pl.ds(c, tm, stride=wr), :]
        lanes = slice(c * V7X_LANES, (c + 1) * V7X_LANES)
        o_ref[:, lanes] = lax.bitcast_convert_type(w << 16, F32).astype(BF16)
        o_ref[:, half + c * V7X_LANES:half + (c + 1) * V7X_LANES] = (
            lax.bitcast_convert_type(w & jnp.uint32(0xFFFF0000), F32).astype(BF16))


def _gather_rows(x, n_tok, tok, tm):
    r_total = tok.shape[0]
    wr = x.shape[0] // n_tok
    d = 2 * wr * V7X_LANES
    return pl.pallas_call(
        functools.partial(_gather_body, tm=tm, wr=wr),
        grid_spec=pltpu.PrefetchScalarGridSpec(
            num_scalar_prefetch=1,
            grid=(r_total // tm,),
            in_specs=[pl.BlockSpec(memory_space=pl.ANY)],
            out_specs=pl.BlockSpec((tm, d), lambda i, tok: (i, 0)),
            scratch_shapes=[pltpu.VMEM((2, tm * wr, V7X_LANES), x.dtype),
                            pltpu.SemaphoreType.DMA((2,))]),
        out_shape=jax.ShapeDtypeStruct((r_total, d), BF16),
        compiler_params=_cparams(("arbitrary",)),
        name="moe_gather",
    )(tok, x)


def _swiglu_chunk(h, sel):
    gate = jnp.minimum(h, SWIGLU_LIMIT)
    up = jnp.clip(h, -SWIGLU_LIMIT, SWIGLU_LIMIT)
    glu = gate * _sigmoid(SWIGLU_ALPHA * gate)
    halves = []
    for c in range(2):
        sl = slice(c * V7X_LANES, (c + 1) * V7X_LANES)
        up_next = pltpu.roll(up[:, sl], V7X_LANES - 1, axis=1)
        halves.append(((up_next + 1.0) * glu[:, sl]).astype(BF16))
    prod = jnp.concatenate(halves, axis=1)
    return jnp.dot(prod, sel, preferred_element_type=F32).astype(BF16)


def _expert_mm_body(bstart_ref, nblk_ref, x_hbm, w_ref, b_ref, *rest, tm, swiglu, n_blocks):
    if swiglu:
        sel_ref, o_hbm, wbf_ref, xbuf, obuf, xsem, osem = rest
    else:
        o_hbm, wbf_ref, xbuf, obuf, xsem, osem = rest
    e = pl.program_id(0)
    n = pl.program_id(1)
    n_tiles = pl.num_programs(1)
    nb = nblk_ref[e]
    b0 = bstart_ref[e]
    otn = obuf.shape[1]
    col0 = pl.multiple_of(n * otn, otn)
    chunk = 2 * V7X_LANES
    n_chunks = wbf_ref.shape[1] // chunk
    sgb = SUPER_BLOCKS
    pieces = [p for p in (16, 8, 4, 2, 1) if p <= sgb]
    assert pieces[0] == sgb, "SUPER_BLOCKS must be a power of two"
    n_sg = (nb + sgb - 1) // sgb
    defer = (n_sg == 1) & (e < pl.num_programs(0) - 1)

    def x_block_copy(blk, q):
        r = pl.multiple_of((b0 + blk) * tm, tm)
        return pltpu.make_async_copy(x_hbm.at[pl.ds(r, tm), :], xbuf.at[pl.ds(q * tm, tm), :], xsem.at[0])

    def o_copy(blk, q, blocks):
        r = pl.multiple_of((b0 + blk) * tm, tm)
        return pltpu.make_async_copy(obuf.at[pl.ds(q * tm, blocks * tm), :],
                                     o_hbm.at[pl.ds(r, blocks * tm), pl.ds(col0, otn)], osem.at[0])

    def piece_layout(cnt):
        out, q, positions = [], 0, [0]
        for p in pieces:
            has = (cnt & p) != 0
            out.append((p, q, has, sorted(s for s in set(positions) if s + p <= sgb)))
            q = q + jnp.where(has, p, 0)
            positions = positions + [s + p for s in positions]
        return out

    def wait_results(cnt):
        for p, _, has, _ in piece_layout(cnt):
            @pl.when(has)
            def _(p=p):
                o_copy(0, 0, p).wait()

    def compute(q, blocks):
        rows = slice(q * tm, (q + blocks) * tm)
        x = xbuf[rows, :]

        def column_chunk(c):
            cs = slice(c * chunk, (c + 1) * chunk)
            return jnp.dot(x, wbf_ref[:, cs], preferred_element_type=F32) + b_ref[:, cs]

        h = column_chunk(0)
        for c in range(n_chunks):
            h_next = column_chunk(c + 1) if c + 1 < n_chunks else None
            if swiglu:
                obuf[rows, c * V7X_LANES:(c + 1) * V7X_LANES] = _swiglu_chunk(h, sel_ref[...])
            else:
                obuf[rows, c * chunk:(c + 1) * chunk] = h
            h = h_next

    @pl.when(nb > 0)
    def _():
        _cast_weight(w_ref, wbf_ref)

    @pl.when(defer & (n > 0))
    def _():
        wait_results(nb)

    def super_group(sg, carry):
        blk0 = sg * sgb
        cnt = jnp.minimum(sgb, nb - blk0)

        @pl.when((n == 0) | (n_sg > 1))
        def _():
            for q in range(sgb):
                @pl.when(q < cnt)
                def _(q=q):
                    x_block_copy(blk0 + q, q).start()
            for q in range(sgb):
                @pl.when(q < cnt)
                def _(q=q):
                    x_block_copy(blk0 + q, q).wait()

        for p, qpos, has, q_options in piece_layout(cnt):
            for q_static in q_options:
                @pl.when(has & (qpos == q_static))
                def _(p=p, q_static=q_static):
                    compute(q_static, p)
                    o_copy(blk0 + q_static, q_static, p).start()

        @pl.when(jnp.logical_not(defer) | (n == n_tiles - 1))
        def _():
            wait_results(cnt)
        return carry

    lax.fori_loop(0, n_sg, super_group, 0)

    @pl.when(e == pl.num_programs(0) - 1)
    def _():
        obuf[0:tm, :] = jnp.zeros((tm, otn), obuf.dtype)

        def zstart(blk, carry):
            o_copy(blk, 0, 1).start()
            return carry

        def zwait(blk, carry):
            o_copy(blk, 0, 1).wait()
            return carry

        lax.fori_loop(nb, n_blocks - b0, zstart, 0)
        lax.fori_loop(nb, n_blocks - b0, zwait, 0)


def _expert_mm(x, w, bias, bstart, nblk, tm, tn, swiglu):
    r_total, k = x.shape
    n_e, _, n_total = w.shape
    out_tn = tn // 2 if swiglu else tn
    out_n = n_total // 2 if swiglu else n_total
    out_dtype = BF16 if swiglu else F32
    in_specs = [pl.BlockSpec(memory_space=pl.ANY),
                pl.BlockSpec((None, k, tn), lambda e, n, bs, nb: (e, 0, n)),
                pl.BlockSpec((None, 1, tn), lambda e, n, bs, nb: (e, 0, n))]
    args = [x, w, bias.reshape(n_e, 1, n_total)]
    if swiglu:
        sel = (lax.broadcasted_iota(jnp.int32, (2 * V7X_LANES, V7X_LANES), 0)
               == 2 * lax.broadcasted_iota(jnp.int32, (2 * V7X_LANES, V7X_LANES), 1)).astype(BF16)
        in_specs.append(pl.BlockSpec((2 * V7X_LANES, V7X_LANES), lambda e, n, bs, nb: (0, 0)))
        args.append(sel)
    return pl.pallas_call(
        functools.partial(_expert_mm_body, tm=tm, swiglu=swiglu, n_blocks=r_total // tm),
        grid_spec=pltpu.PrefetchScalarGridSpec(
            num_scalar_prefetch=2,
            grid=(n_e, n_total // tn),
            in_specs=in_specs,
            out_specs=pl.BlockSpec(memory_space=pl.ANY),
            scratch_shapes=[pltpu.VMEM((k, tn), BF16),
                            pltpu.VMEM((SUPER_BLOCKS * tm, k), BF16),
                            pltpu.VMEM((SUPER_BLOCKS * tm, out_tn), out_dtype),
                            pltpu.SemaphoreType.DMA((1,)),
                            pltpu.SemaphoreType.DMA((1,))]),
        out_shape=jax.ShapeDtypeStruct((r_total, out_n), out_dtype),
        compiler_params=_cparams(("arbitrary", "arbitrary")),
        name="moe_gate_up" if swiglu else "moe_down",
    )(bstart, nblk, *args)


def _combine_rows_copy(src_hbm, buf_ref, sem_ref, row, slot, k, r):
    return pltpu.make_async_copy(src_hbm.at[pl.ds(row, 1), :],
                                 buf_ref.at[slot, k, pl.ds(r, 1), :],
                                 sem_ref.at[slot])


def _combine_body(dest_ref, yb_hbm, x_ref, gate_ref, g_ref, o_ref, buf_ref, sem_ref, *, tm):
    i = pl.program_id(0)
    n = pl.num_programs(0)

    def issue(block, slot):
        def body(r, carry):
            for k in range(TOP_K):
                row = dest_ref[(block * tm + r) * TOP_K + k]
                _combine_rows_copy(yb_hbm, buf_ref, sem_ref, row, slot, k, r).start()
            return carry
        lax.fori_loop(0, tm, body, 0, unroll=2)

    @pl.when(i == 0)
    def _():
        issue(0, 0)

    @pl.when(i + 1 < n)
    def _():
        issue(i + 1, (i + 1) % 2)

    slot = i % 2
    for k in range(TOP_K):
        pltpu.make_async_copy(yb_hbm.at[pl.ds(0, tm), :], buf_ref.at[slot, k], sem_ref.at[slot]).wait()
    gates = gate_ref[...]
    acc = x_ref[...]
    for k in range(TOP_K):
        acc = acc + gates[:, k:k + 1] * buf_ref[slot, k]
    ms = jnp.mean(acc * acc, axis=-1, keepdims=True)
    o_ref[...] = acc * lax.rsqrt(ms + RMS_EPS) * g_ref[...]


def _combine(yb, dest_flat, x, gates, g, tm):
    t, d = x.shape
    return pl.pallas_call(
        functools.partial(_combine_body, tm=tm),
        grid_spec=pltpu.PrefetchScalarGridSpec(
            num_scalar_prefetch=1,
            grid=(t // tm,),
            in_specs=[pl.BlockSpec(memory_space=pl.ANY),
                      pl.BlockSpec((tm, d), lambda i, dest: (i, 0)),
                      pl.BlockSpec((tm, V7X_LANES), lambda i, dest: (i, 0)),
                      pl.BlockSpec((1, d), lambda i, dest: (0, 0))],
            out_specs=pl.BlockSpec((tm, d), lambda i, dest: (i, 0)),
            scratch_shapes=[pltpu.VMEM((2, TOP_K, tm, d), F32), pltpu.SemaphoreType.DMA((2,))]),
        out_shape=jax.ShapeDtypeStruct((t, d), F32),
        compiler_params=_cparams(("arbitrary",)),
        name="moe_combine",
    )(dest_flat, yb, x, gates, g.reshape(1, d))


def _routing_tables(top_idx, n_e, tm):
    t = top_idx.shape[0]
    p = t * TOP_K
    flat_e = top_idx.reshape(p)
    onehot = (flat_e[:, None] == jnp.arange(n_e, dtype=jnp.int32)[None, :]).astype(jnp.int32)
    rank = jnp.take_along_axis(jnp.cumsum(onehot, axis=0), flat_e[:, None], axis=1)[:, 0] - 1
    counts = jnp.sum(onehot, axis=0)
    pcounts = (counts + tm - 1) // tm * tm
    pends = jnp.cumsum(pcounts)
    pstarts = pends - pcounts
    dest = (pstarts[flat_e] + rank).astype(jnp.int32)
    n_blocks = p // tm + n_e
    flat_tok = jnp.arange(p, dtype=jnp.int32) // TOP_K
    slot_tok = jnp.zeros((n_blocks * tm,), jnp.int32).at[dest].set(flat_tok)
    return dest, slot_tok, (pstarts // tm).astype(jnp.int32), (pcounts // tm).astype(jnp.int32)


def kernel(x_prompt, x_sample, state_s5_re, state_s5_im, state_hgrn, norm_mix, w_in, b_gate, s5_lam_re, s5_lam_im, s5_log_step, s5_b_re, s5_b_im, s5_c_re, s5_c_im, s5_d, s5_w_glu, hgrn_lb_logits, hgrn_norm, w_branch_s5, w_branch_hgrn, w_out, norm_ffn, w_router, b_router, w_gate_up, b_gate_up, w_down, b_down, norm_final):
    depth = w_in.shape[0]
    assert depth == 1, "single-layer stack"
    n_p, s_p, d = x_prompt.shape
    n_s, s_s, _ = x_sample.shape
    t_p, t_s = n_p * s_p, n_s * s_s
    g_s5, p_s5 = s5_lam_re.shape[1:]
    d_s5 = g_s5 * S5_GROUP
    n_heads = state_hgrn.shape[2]
    d_h = n_heads * HGRN_HEAD_DIM
    n_e = w_router.shape[2]
    l = 0

    lb_all = jnp.cumsum(jax.nn.softmax(hgrn_lb_logits.astype(F32), axis=0), axis=0)[:depth]
    x = jnp.concatenate([x_prompt.reshape(t_p, d), x_sample.reshape(t_s, d)], axis=0)
    t = t_p + t_s

    h = _rmsnorm(x, norm_mix[l], BF16)
    z = _dense_mm([(h, 0)], [w_in[l]], [], lambda accs, ex: accs[0], F32, tm=1024, tn=512, name="in_proj")

    wb, wc, lam = _s5_params(s5_lam_re[l], s5_lam_im[l], s5_log_step[l], s5_b_re[l], s5_b_im[l],
                             s5_c_re[l], s5_c_im[l])
    d_flat = s5_d[l].reshape(1, d_s5)
    zeros_s5 = jnp.zeros((n_p, g_s5 * p_s5), F32)
    y_p, p_re, p_im = _s5_scan(z, 0, n_p, s_p, zeros_s5, zeros_s5, wb, wc, lam, d_flat, part_rows=256)
    y_s, s_re, s_im = _s5_scan(z, t_p, n_s, s_s, state_s5_re[l].reshape(n_s, -1),
                               state_s5_im[l].reshape(n_s, -1), wb, wc, lam, d_flat, part_rows=256)
    y = jnp.concatenate([y_p, y_s], axis=0)
    a = _dense_mm([(y, 0)], [s5_w_glu[l]],
                  [(y, (1024, 512), lambda n, m: (m, n))],
                  lambda accs, ex: ex[0] * _sigmoid(accs[0]), BF16, tm=1024, tn=512, name="s5_glu")

    lb_flat = lb_all[l].reshape(1, d_h)
    cb0 = d_s5 // d_h
    zeros_hg = jnp.zeros((n_p,) + state_hgrn.shape[2:], F32)
    o_p, p_hg = _hgrn(z, 0, n_p, s_p, zeros_hg, lb_flat, hgrn_norm[l], cb0)
    o_s, s_hg = _hgrn(z, t_p, n_s, s_s, state_hgrn[l], lb_flat, hgrn_norm[l], cb0)
    b = jnp.concatenate([o_p, o_s], axis=0)

    tn = 512
    ga_cb = (d_s5 + 4 * d_h) // tn
    gb_cb = ga_cb + d // tn
    bg = b_gate[l].reshape(1, 2 * d)
    merged = _dense_mm(
        [(a, 0), (b, 0)], [w_branch_s5[l], w_branch_hgrn[l]],
        [(z, (512, tn), lambda n, m: (m, ga_cb + n)),
         (z, (512, tn), lambda n, m: (m, gb_cb + n)),
         (bg, (1, tn), lambda n, m: (0, n)),
         (bg, (1, tn), lambda n, m: (0, d // tn + n))],
        lambda accs, ex: (_sigmoid(ex[0] + ex[2]) * accs[0] + _sigmoid(ex[1] + ex[3]) * accs[1]),
        BF16, tm=512, tn=tn, name="merge")
    x1 = _dense_mm([(merged, 0)], [w_out[l]], [(x, (1024, 512), lambda n, m: (m, n))],
                   lambda accs, ex: ex[0] + accs[0], F32, tm=1024, tn=512, name="out_proj")

    tm_e = 256
    xn, idx_wide, gate_wide = _router(x1, norm_ffn[l], w_router[l], b_router[l])
    top_idx = idx_wide[:, :TOP_K]
    dest, slot_tok, bstart, nblk = _routing_tables(top_idx, n_e, tm_e)
    xs = _gather_rows(xn, t, slot_tok, tm_e)
    act = _expert_mm(xs, w_gate_up[l], b_gate_up[l], bstart, nblk, tm_e, 512, swiglu=True)
    yb = _expert_mm(act, w_down[l], b_down[l], bstart, nblk, tm_e, 512, swiglu=False)
    y_out = _combine(yb, dest, x1, gate_wide, norm_final, tm=128)

    y_prompt = y_out[:t_p].reshape(n_p, s_p, d)
    y_sample = y_out[t_p:].reshape(n_s, s_s, d)
    return (y_prompt, y_sample,
            p_re.reshape(1, n_p, g_s5, p_s5), p_im.reshape(1, n_p, g_s5, p_s5), p_hg[None],
            s_re.reshape(1, n_s, g_s5, p_s5), s_im.reshape(1, n_s, g_s5, p_s5), s_hg[None])
```

```python
import functools

import jax
import jax.numpy as jnp
from jax import lax
from jax.experimental import pallas as pl
from jax.experimental.pallas import tpu as pltpu

F32 = jnp.float32
BF16 = jnp.bfloat16

RMS_EPS = 1e-5
S5_GROUP = 16
S5_STATE = 64
S5_GROUPS_PER_BLOCK = 16
HGRN_HEAD_DIM = 128
HGRN_CHUNK = 32
TOP_K = 4
SWIGLU_LIMIT = 7.0
SWIGLU_ALPHA = 1.702
EXP_CLAMP = 80.0

V7X_LANES = 128
V7X_SUBLANES = 8
V7X_BF16_SUBLANES = 16
V7X_VMEM_LIMIT = 56 * 1024 * 1024
CAST_ROWS = 256
SUPER_BLOCKS = 8
ISSUE_UNROLL = 8


def _cparams(semantics, vmem=V7X_VMEM_LIMIT):
    return pltpu.CompilerParams(dimension_semantics=semantics, vmem_limit_bytes=vmem)


def _sigmoid(x):
    return 1.0 / (1.0 + jnp.exp(-x))


def _cast_weight(w_ref, wbf_ref, row0=0):
    k = w_ref.shape[0]
    rows = min(CAST_ROWS, k)

    def body(i, carry):
        r = pl.multiple_of(i * rows, rows)
        wbf_ref[pl.ds(row0 + r, rows), :] = w_ref[pl.ds(r, rows), :].astype(BF16)
        return carry

    lax.fori_loop(0, k // rows, body, 0)


def _rmsnorm_body(x_ref, g_ref, o_ref):
    x = x_ref[...]
    ms = jnp.mean(x * x, axis=-1, keepdims=True)
    o_ref[...] = (x * lax.rsqrt(ms + RMS_EPS) * g_ref[...]).astype(o_ref.dtype)


def _rmsnorm(x, g, out_dtype, tm=256):
    t, d = x.shape
    return pl.pallas_call(
        _rmsnorm_body,
        grid=(t // tm,),
        in_specs=[pl.BlockSpec((tm, d), lambda i: (i, 0)),
                  pl.BlockSpec((1, d), lambda i: (0, 0))],
        out_specs=pl.BlockSpec((tm, d), lambda i: (i, 0)),
        out_shape=jax.ShapeDtypeStruct((t, d), out_dtype),
        compiler_params=_cparams(("arbitrary",)),
        name="rmsnorm",
    )(x, g.reshape(1, d))


def _mm_body(*refs, n_x, n_extra, epilogue):
    x_refs = refs[:n_x]
    w_refs = refs[n_x:2 * n_x]
    extra_refs = refs[2 * n_x:2 * n_x + n_extra]
    o_ref = refs[2 * n_x + n_extra]
    wbf_refs = refs[2 * n_x + n_extra + 1:]

    @pl.when(pl.program_id(1) == 0)
    def _():
        for w_ref, wbf_ref in zip(w_refs, wbf_refs):
            _cast_weight(w_ref, wbf_ref)

    accs = [jnp.dot(x_ref[...].astype(BF16), wbf_ref[...], preferred_element_type=F32)
            for x_ref, wbf_ref in zip(x_refs, wbf_refs)]
    o_ref[...] = epilogue(accs, [e[...] for e in extra_refs]).astype(o_ref.dtype)


def _dense_mm(xs, ws, extras, epilogue, out_dtype, tm, tn, name):
    m_total = xs[0][0].shape[0]
    n_total = ws[0].shape[1]
    in_specs = []
    for (x, cb), w in zip(xs, ws):
        in_specs.append(pl.BlockSpec((tm, w.shape[0]), functools.partial(lambda n, m, cb: (m, cb), cb=cb)))
    for w in ws:
        in_specs.append(pl.BlockSpec((w.shape[0], tn), lambda n, m: (0, n)))
    for _, bshape, imap in extras:
        in_specs.append(pl.BlockSpec(bshape, imap))
    body = functools.partial(_mm_body, n_x=len(xs), n_extra=len(extras), epilogue=epilogue)
    return pl.pallas_call(
        body,
        grid=(n_total // tn, m_total // tm),
        in_specs=in_specs,
        out_specs=pl.BlockSpec((tm, tn), lambda n, m: (m, n)),
        out_shape=jax.ShapeDtypeStruct((m_total, n_total), out_dtype),
        scratch_shapes=[pltpu.VMEM((w.shape[0], tn), BF16) for w in ws],
        compiler_params=_cparams(("arbitrary", "arbitrary")),
        name=name,
    )(*[x for x, _ in xs], *ws, *[e for e, _, _ in extras])


def _s5_params(lam_re, lam_im, log_step, b_re, b_im, c_re, c_im):
    g, p = lam_re.shape
    h = b_re.shape[-1]
    gb = S5_GROUPS_PER_BLOCK
    nb = g // gb
    dt = jnp.exp(log_step)[:, None]
    mag = jnp.exp(lam_re * dt)
    ang = lam_im * dt
    lbar_re, lbar_im = mag * jnp.cos(ang), mag * jnp.sin(ang)
    nr, ni = lbar_re - 1.0, lbar_im
    den = lam_re * lam_re + lam_im * lam_im
    fr = (nr * lam_re + ni * lam_im) / den
    fi = (ni * lam_re - nr * lam_im) / den
    bb_re = fr[:, :, None] * b_re - fi[:, :, None] * b_im
    bb_im = fr[:, :, None] * b_im + fi[:, :, None] * b_re
    eye = jnp.eye(gb, dtype=F32)

    def blockdiag_in(bb):
        t = bb.reshape(nb, gb, p, h)
        return jnp.einsum('jgph,gk->jghkp', t, eye).reshape(nb, gb * h, gb * p)

    def blockdiag_out(c):
        t = c.reshape(nb, gb, h, p)
        return jnp.einsum('jghp,gk->jgpkh', t, eye).reshape(nb, gb * p, gb * h)

    wb = jnp.concatenate([blockdiag_in(bb_re), blockdiag_in(bb_im)], axis=2).astype(BF16)
    wc = jnp.concatenate([blockdiag_out(c_re), -blockdiag_out(c_im)], axis=1).astype(BF16)
    lam = jnp.concatenate([lbar_re.reshape(nb, 1, gb * p), lbar_im.reshape(nb, 1, gb * p)], axis=2)
    return wb, wc, lam


def _s5_body(*refs, n_parts, nsb, seq_rows, spb, part_pitch, seq_pitch):
    u_refs = refs[:n_parts]
    (sre_ref, sim_ref, wb_ref, wc_ref, lam_ref, d_ref,
     y_ref, nre_ref, nim_ref, bu_ref, xs_ref, cre_ref, cim_ref) = refs[n_parts:]
    hs = cre_ref.shape[1]
    pr = u_refs[0].shape[0]
    nc = hs // V7X_LANES
    for p, u_ref in enumerate(u_refs):
        bu = jnp.dot(u_ref[...].astype(BF16), wb_ref[...], preferred_element_type=F32)
        for c in range(2 * nc):
            bu_ref[c, p * part_pitch:p * part_pitch + pr, :] = bu[:, c * V7X_LANES:(c + 1) * V7X_LANES]

    @pl.when(pl.program_id(1) % spb == 0)
    def _():
        cre_ref[...] = sre_ref[...]
        cim_ref[...] = sim_ref[...]

    def lanes(c):
        return slice(c * V7X_LANES, (c + 1) * V7X_LANES)

    for g0 in range(0, nsb, V7X_SUBLANES):
        ns = min(V7X_SUBLANES, nsb - g0)
        lr = [jnp.broadcast_to(lam_ref[:, lanes(c)], (ns, V7X_LANES)) for c in range(nc)]
        li = [jnp.broadcast_to(lam_ref[:, lanes(nc + c)], (ns, V7X_LANES)) for c in range(nc)]

        def t_body(t, carry, g0=g0, ns=ns, lr=lr, li=li):
            xr, xi = carry
            rows = pl.ds(g0 * seq_pitch + t, ns, stride=seq_pitch)
            nr, ni = [], []
            for c in range(nc):
                r = lr[c] * xr[c] - li[c] * xi[c] + bu_ref[c, rows, :]
                i = lr[c] * xi[c] + li[c] * xr[c] + bu_ref[nc + c, rows, :]
                xs_ref[c, rows, :] = r
                xs_ref[nc + c, rows, :] = i
                nr.append(r)
                ni.append(i)
            return tuple(nr), tuple(ni)

        init = (tuple(cre_ref[g0:g0 + ns, lanes(c)] for c in range(nc)),
                tuple(cim_ref[g0:g0 + ns, lanes(c)] for c in range(nc)))
        xr, xi = lax.fori_loop(0, seq_rows, t_body, init, unroll=8)
        for c in range(nc):
            cre_ref[g0:g0 + ns, lanes(c)] = xr[c]
            cim_ref[g0:g0 + ns, lanes(c)] = xi[c]

    for p, u_ref in enumerate(u_refs):
        xs = jnp.concatenate([xs_ref[c, p * part_pitch:p * part_pitch + pr, :].astype(BF16)
                              for c in range(2 * nc)], axis=1)
        y = jnp.dot(xs, wc_ref[...], preferred_element_type=F32) + d_ref[...] * u_ref[...]
        y_ref[p] = jax.nn.gelu(y)
    nre_ref[...] = cre_ref[...]
    nim_ref[...] = cim_ref[...]


def _s5_scan(z, row0, n_seq, seq_len, st_re, st_im, wb, wc, lam, d_flat, part_rows):
    nb, bw, sw = wb.shape
    hs = sw // 2
    if seq_len >= part_rows:
        n_parts, nsb, seq_rows = n_seq, n_seq, part_rows
        n_rb = seq_len // part_rows
        spb = n_rb
    else:
        n_parts, nsb, seq_rows = 1, part_rows // seq_len, seq_len
        n_rb = n_seq * seq_len // part_rows
        spb = 1
    n_sblk = n_rb // spb
    rb0 = row0 // part_rows
    part_stride = seq_len // part_rows if n_parts > 1 else 0
    st_re3 = st_re.reshape(n_sblk, nsb, nb * hs)
    st_im3 = st_im.reshape(n_sblk, nsb, nb * hs)
    part_pitch = part_rows + V7X_SUBLANES if n_parts > 1 else part_rows
    seq_pitch = part_pitch if n_parts > 1 else seq_rows
    body = functools.partial(_s5_body, n_parts=n_parts, nsb=nsb, seq_rows=seq_rows, spb=spb,
                             part_pitch=part_pitch, seq_pitch=seq_pitch)
    state_spec = pl.BlockSpec((None, nsb, hs), lambda j, r: (r // spb, 0, j))
    u_specs = [pl.BlockSpec((part_rows, bw),
                            functools.partial(lambda j, r, p: (rb0 + p * part_stride + r, j), p=p))
               for p in range(n_parts)]
    rows_per_part = n_seq * seq_len // n_parts
    y, nre, nim = pl.pallas_call(
        body,
        grid=(nb, n_rb),
        in_specs=u_specs + [
            state_spec, state_spec,
            pl.BlockSpec((None, bw, sw), lambda j, r: (j, 0, 0)),
            pl.BlockSpec((None, sw, bw), lambda j, r: (j, 0, 0)),
            pl.BlockSpec((None, 1, sw), lambda j, r: (j, 0, 0)),
            pl.BlockSpec((1, bw), lambda j, r: (0, j))],
        out_specs=[pl.BlockSpec((n_parts, part_rows, bw), lambda j, r: (0, r, j)),
                   state_spec, state_spec],
        out_shape=[jax.ShapeDtypeStruct((n_parts, rows_per_part, nb * bw), F32),
                   jax.ShapeDtypeStruct((n_sblk, nsb, nb * hs), F32),
                   jax.ShapeDtypeStruct((n_sblk, nsb, nb * hs), F32)],
        scratch_shapes=[pltpu.VMEM((sw // V7X_LANES, n_parts * part_pitch, V7X_LANES), F32),
                        pltpu.VMEM((sw // V7X_LANES, n_parts * part_pitch, V7X_LANES), F32),
                        pltpu.VMEM((nsb, hs), F32),
                        pltpu.VMEM((nsb, hs), F32)],
        compiler_params=_cparams(("arbitrary", "arbitrary")),
        name="s5_scan",
    )(*([z] * n_parts), st_re3, st_im3, wb, wc, lam, d_flat)
    return (y.reshape(n_seq * seq_len, nb * bw),
            nre.reshape(n_seq, nb * hs), nim.reshape(n_seq, nb * hs))


def _cumsum_rows(x, seg):
    row = lax.broadcasted_iota(jnp.int32, x.shape, 0) % seg
    s = 1
    while s < seg:
        x = x + jnp.where(row >= s, pltpu.roll(x, s, axis=0), 0.0)
        s *= 2
    return x


def _hgrn_body(q_ref, f_ref, v_ref, g_ref, st_ref, lb_ref, ng_ref, o_ref, ns_ref, s_ref,
               *, rows, prows, n_chunks, n_heads):
    kd = HGRN_HEAD_DIM
    c = pl.program_id(1)

    @pl.when(c == 0)
    def _():
        s_ref[...] = st_ref[...]

    lb = lb_ref[...]
    fz = f_ref[...]
    q = q_ref[...]
    log_f = jnp.log(lb + (1.0 - lb) * _sigmoid(fz))
    kk = (1.0 - lb) * _sigmoid(-fz)
    qh = q * _sigmoid(q) * (kd ** -0.5)
    v = v_ref[...]
    if prows > rows:
        pad = jnp.zeros((prows - rows, fz.shape[1]), F32)
        log_f, kk, qh, v = [jnp.concatenate([a, pad], axis=0) for a in (log_f, kk, qh, v)]
    b = _cumsum_rows(log_f, prows)
    b_end = b[rows - 1:rows, :]
    b_mid = b[rows // 2 - 1:rows // 2, :]
    qt = (qh * jnp.exp(jnp.minimum(b - b_mid, EXP_CLAMP))).astype(BF16)
    kt = (kk * jnp.exp(jnp.minimum(b_mid - b, EXP_CLAMP))).astype(BF16)
    qe = (qh * jnp.exp(b)).astype(BF16)
    ke = (kk * jnp.exp(b_end - b)).astype(BF16)
    dec = jnp.exp(b_end)
    vb = v.astype(BF16)
    causal = (lax.broadcasted_iota(jnp.int32, (prows, prows), 0)
              >= lax.broadcasted_iota(jnp.int32, (prows, prows), 1))
    eye = (lax.broadcasted_iota(jnp.int32, (kd, kd), 0)
           == lax.broadcasted_iota(jnp.int32, (kd, kd), 1))
    og = g_ref[...]
    gate = og * _sigmoid(og)
    ng = ng_ref[...]
    for h in range(n_heads):
        sl = slice(h * kd, (h + 1) * kd)
        sc = lax.dot_general(qt[:, sl], kt[:, sl], (((1,), (1,)), ((), ())),
                             preferred_element_type=F32)
        sc = jnp.where(causal, sc, 0.0).astype(BF16)
        s_prev = s_ref[h]
        o = (jnp.dot(sc, vb[:, sl], preferred_element_type=F32)
             + jnp.dot(qe[:, sl], s_prev.astype(BF16), preferred_element_type=F32))
        upd = lax.dot_general(ke[:, sl], vb[:, sl], (((0,), (0,)), ((), ())),
                              preferred_element_type=F32)
        dec_col = jnp.sum(jnp.where(eye, jnp.broadcast_to(dec[:, sl], (kd, kd)), 0.0),
                          axis=1, keepdims=True)
        s_ref[h] = dec_col * s_prev + upd
        o = o[:rows]
        o = o * lax.rsqrt(jnp.mean(o * o, axis=-1, keepdims=True) + RMS_EPS) * ng
        o_ref[:, sl] = o * gate[:, sl]

    @pl.when(c == n_chunks - 1)
    def _():
        ns_ref[...] = s_ref[...]


def _hgrn(z, row0, n_seq, seq_len, state, lb_flat, norm_g, col_block0):
    n_heads, kd = state.shape[1], state.shape[2]
    dh = n_heads * kd
    rows = min(HGRN_CHUNK, seq_len)
    prows = max(rows, V7X_BF16_SUBLANES)
    n_chunks = seq_len // rows
    rb0 = row0 // rows
    body = functools.partial(_hgrn_body, rows=rows, prows=prows, n_chunks=n_chunks, n_heads=n_heads)

    def zspec(i):
        return pl.BlockSpec((rows, dh), lambda s, c: (rb0 + s * n_chunks + c, col_block0 + i))

    st_spec = pl.BlockSpec((None, n_heads, kd, kd), lambda s, c: (s, 0, 0, 0))
    o, ns = pl.pallas_call(
        body,
        grid=(n_seq, n_chunks),
        in_specs=[zspec(0), zspec(1), zspec(2), zspec(3), st_spec,
                  pl.BlockSpec((1, dh), lambda s, c: (0, 0)),
                  pl.BlockSpec((1, kd), lambda s, c: (0, 0))],
        out_specs=[pl.BlockSpec((rows, dh), lambda s, c: (s * n_chunks + c, 0)), st_spec],
        out_shape=[jax.ShapeDtypeStruct((n_seq * seq_len, dh), F32),
                   jax.ShapeDtypeStruct(state.shape, F32)],
        scratch_shapes=[pltpu.VMEM((n_heads, kd, kd), F32)],
        compiler_params=_cparams(("arbitrary", "arbitrary")),
        name="hgrn",
    )(z, z, z, z, state, lb_flat, norm_g.reshape(1, kd))
    return o, ns


def _router_body(x_ref, g_ref, w_ref, b_ref, xn_ref, idx_ref, gate_ref):
    x = x_ref[...]
    ms = jnp.mean(x * x, axis=-1, keepdims=True)
    xn = x * lax.rsqrt(ms + RMS_EPS) * g_ref[...]
    tm, d = xn.shape
    half = d // 2
    lo = lax.bitcast_convert_type(xn[:, :half].astype(BF16).astype(F32), jnp.uint32)
    hi = lax.bitcast_convert_type(xn[:, half:].astype(BF16).astype(F32), jnp.uint32)
    packed = (lo >> 16) | (hi & jnp.uint32(0xFFFF0000))
    wr = half // V7X_LANES
    for c in range(wr):
        xn_ref[pl.ds(c, tm, stride=wr), :] = packed[:, c * V7X_LANES:(c + 1) * V7X_LANES]
    logits = jnp.dot(xn.astype(BF16), w_ref[...].astype(BF16), preferred_element_type=F32) + b_ref[...]
    n_e = logits.shape[1]
    lane = lax.broadcasted_iota(jnp.int32, logits.shape, 1)
    out_lane = lax.broadcasted_iota(jnp.int32, idx_ref.shape, 1)
    idx_out = jnp.zeros(idx_ref.shape, jnp.int32)
    val_out = jnp.full(gate_ref.shape, -jnp.inf, F32)
    work = logits
    for k in range(TOP_K):
        m = jnp.max(work, axis=-1, keepdims=True)
        i = jnp.min(jnp.where(work == m, lane, n_e), axis=-1, keepdims=True)
        idx_out = jnp.where(out_lane == k, i, idx_out)
        val_out = jnp.where(out_lane == k, m, val_out)
        work = jnp.where(lane == i, -jnp.inf, work)
    top = jnp.max(val_out, axis=-1, keepdims=True)
    e = jnp.exp(val_out - top)
    gate_ref[...] = e / jnp.sum(e, axis=-1, keepdims=True)
    idx_ref[...] = idx_out


def _router(x, g, w_router, b_router, tm=256):
    t, d = x.shape
    n_e = w_router.shape[1]
    row = lambda i: (i, 0)
    fixed = lambda i: (0, 0)
    return pl.pallas_call(
        _router_body,
        grid=(t // tm,),
        in_specs=[pl.BlockSpec((tm, d), row), pl.BlockSpec((1, d), fixed),
                  pl.BlockSpec((d, n_e), fixed), pl.BlockSpec((1, n_e), fixed)],
        out_specs=[pl.BlockSpec((tm * (d // 2 // V7X_LANES), V7X_LANES), row),
                   pl.BlockSpec((tm, V7X_LANES), row), pl.BlockSpec((tm, V7X_LANES), row)],
        out_shape=[jax.ShapeDtypeStruct((t * (d // 2 // V7X_LANES), V7X_LANES), jnp.uint32),
                   jax.ShapeDtypeStruct((t, V7X_LANES), jnp.int32),
                   jax.ShapeDtypeStruct((t, V7X_LANES), F32)],
        compiler_params=_cparams(("arbitrary",)),
        name="router",
    )(x, g.reshape(1, d), w_router, b_router.reshape(1, n_e))


def _gather_rows_copy(src_hbm, buf_ref, sem_ref, tok, slot, r, wr):
    return pltpu.make_async_copy(src_hbm.at[pl.ds(pl.multiple_of(tok * wr, wr), wr), :],
                                 buf_ref.at[slot, pl.ds(pl.multiple_of(r * wr, wr), wr), :],
                                 sem_ref.at[slot])


def _gather_body(tok_ref, x_hbm, o_ref, buf_ref, sem_ref, *, tm, wr):
    i = pl.program_id(0)
    n = pl.num_programs(0)

    def issue(block, slot):
        def body(r, carry):
            _gather_rows_copy(x_hbm, buf_ref, sem_ref, tok_ref[block * tm + r], slot, r, wr).start()
            return carry
        lax.fori_loop(0, tm, body, 0, unroll=ISSUE_UNROLL)

    @pl.when(i == 0)
    def _():
        issue(0, 0)

    @pl.when(i + 1 < n)
    def _():
        issue(i + 1, (i + 1) % 2)

    slot = i % 2
    pltpu.make_async_copy(x_hbm.at[pl.ds(0, tm * wr), :], buf_ref.at[slot], sem_ref.at[slot]).wait()
    half = wr * V7X_LANES
    for c in range(wr):
        w = buf_ref[slot, pl.ds(c, tm, stride=wr), :]
        lanes = slice(c * V7X_LANES, (c + 1) * V7X_LANES)
        o_ref[:, lanes] = lax.bitcast_convert_type(w << 16, F32).astype(BF16)
        o_ref[:, half + c * V7X_LANES:half + (c + 1) * V7X_LANES] = (
            lax.bitcast_convert_type(w & jnp.uint32(0xFFFF0000), F32).astype(BF16))


def _gather_rows(x, n_tok, tok, tm):
    r_total = tok.shape[0]
    wr = x.shape[0] // n_tok
    d = 2 * wr * V7X_LANES
    return pl.pallas_call(
        functools.partial(_gather_body, tm=tm, wr=wr),
        grid_spec=pltpu.PrefetchScalarGridSpec(
            num_scalar_prefetch=1,
            grid=(r_total // tm,),
            in_specs=[pl.BlockSpec(memory_space=pl.ANY)],
            out_specs=pl.BlockSpec((tm, d), lambda i, tok: (i, 0)),
            scratch_shapes=[pltpu.VMEM((2, tm * wr, V7X_LANES), x.dtype),
                            pltpu.SemaphoreType.DMA((2,))]),
        out_shape=jax.ShapeDtypeStruct((r_total, d), BF16),
        compiler_params=_cparams(("arbitrary",)),
        name="moe_gather",
    )(tok, x)


def _swiglu_chunk(h, sel):
    gate = jnp.minimum(h, SWIGLU_LIMIT)
    up = jnp.clip(h, -SWIGLU_LIMIT, SWIGLU_LIMIT)
    glu = gate * _sigmoid(SWIGLU_ALPHA * gate)
    halves = []
    for c in range(2):
        sl = slice(c * V7X_LANES, (c + 1) * V7X_LANES)
        up_next = pltpu.roll(up[:, sl], V7X_LANES - 1, axis=1)
        halves.append(((up_next + 1.0) * glu[:, sl]).astype(BF16))
    prod = jnp.concatenate(halves, axis=1)
    return jnp.dot(prod, sel, preferred_element_type=F32).astype(BF16)


def _expert_mm_body(bstart_ref, nblk_ref, x_hbm, w_ref, b_ref, *rest, tm, swiglu, n_blocks):
    if swiglu:
        sel_ref, o_hbm, wbf_ref, xbuf, obuf, xsem, osem = rest
    else:
        o_hbm, wbf_ref, xbuf, obuf, xsem, osem = rest
    e = pl.program_id(0)
    n = pl.program_id(1)
    n_tiles = pl.num_programs(1)
    nb = nblk_ref[e]
    b0 = bstart_ref[e]
    otn = obuf.shape[1]
    col0 = pl.multiple_of(n * otn, otn)
    chunk = 2 * V7X_LANES
    n_chunks = wbf_ref.shape[1] // chunk
    sgb = SUPER_BLOCKS
    pieces = [p for p in (16, 8, 4, 2, 1) if p <= sgb]
    assert pieces[0] == sgb, "SUPER_BLOCKS must be a power of two"
    n_sg = (nb + sgb - 1) // sgb
    defer = (n_sg == 1) & (e < pl.num_programs(0) - 1)

    def x_block_copy(blk, q):
        r = pl.multiple_of((b0 + blk) * tm, tm)
        return pltpu.make_async_copy(x_hbm.at[pl.ds(r, tm), :], xbuf.at[pl.ds(q * tm, tm), :], xsem.at[0])

    def o_copy(blk, q, blocks):
        r = pl.multiple_of((b0 + blk) * tm, tm)
        return pltpu.make_async_copy(obuf.at[pl.ds(q * tm, blocks * tm), :],
                                     o_hbm.at[pl.ds(r, blocks * tm), pl.ds(col0, otn)], osem.at[0])

    def piece_layout(cnt):
        out, q, positions = [], 0, [0]
        for p in pieces:
            has = (cnt & p) != 0
            out.append((p, q, has, sorted(s for s in set(positions) if s + p <= sgb)))
            q = q + jnp.where(has, p, 0)
            positions = positions + [s + p for s in positions]
        return out

    def wait_results(cnt):
        for p, _, has, _ in piece_layout(cnt):
            @pl.when(has)
            def _(p=p):
                o_copy(0, 0, p).wait()

    def compute(q, blocks):
        rows = slice(q * tm, (q + blocks) * tm)
        x = xbuf[rows, :]

        def column_chunk(c):
            cs = slice(c * chunk, (c + 1) * chunk)
            return jnp.dot(x, wbf_ref[:, cs], preferred_element_type=F32) + b_ref[:, cs]

        h = column_chunk(0)
        for c in range(n_chunks):
            h_next = column_chunk(c + 1) if c + 1 < n_chunks else None
            if swiglu:
                obuf[rows, c * V7X_LANES:(c + 1) * V7X_LANES] = _swiglu_chunk(h, sel_ref[...])
            else:
                obuf[rows, c * chunk:(c + 1) * chunk] = h
            h = h_next

    @pl.when(nb > 0)
    def _():
        _cast_weight(w_ref, wbf_ref)

    @pl.when(defer & (n > 0))
    def _():
        wait_results(nb)

    def super_group(sg, carry):
        blk0 = sg * sgb
        cnt = jnp.minimum(sgb, nb - blk0)

        @pl.when((n == 0) | (n_sg > 1))
        def _():
            for q in range(sgb):
                @pl.when(q < cnt)
                def _(q=q):
                    x_block_copy(blk0 + q, q).start()
            for q in range(sgb):
                @pl.when(q < cnt)
                def _(q=q):
                    x_block_copy(blk0 + q, q).wait()

        for p, qpos, has, q_options in piece_layout(cnt):
            for q_static in q_options:
                @pl.when(has & (qpos == q_static))
                def _(p=p, q_static=q_static):
                    compute(q_static, p)
                    o_copy(blk0 + q_static, q_static, p).start()

        @pl.when(jnp.logical_not(defer) | (n == n_tiles - 1))
        def _():
            wait_results(cnt)
        return carry

    lax.fori_loop(0, n_sg, super_group, 0)

    @pl.when(e == pl.num_programs(0) - 1)
    def _():
        obuf[0:tm, :] = jnp.zeros((tm, otn), obuf.dtype)

        def zstart(blk, carry):
            o_copy(blk, 0, 1).start()
            return carry

        def zwait(blk, carry):
            o_copy(blk, 0, 1).wait()
            return carry

        lax.fori_loop(nb, n_blocks - b0, zstart, 0)
        lax.fori_loop(nb, n_blocks - b0, zwait, 0)


def _expert_mm(x, w, bias, bstart, nblk, tm, tn, swiglu):
    r_total, k = x.shape
    n_e, _, n_total = w.shape
    out_tn = tn // 2 if swiglu else tn
    out_n = n_total // 2 if swiglu else n_total
    out_dtype = BF16 if swiglu else F32
    in_specs = [pl.BlockSpec(memory_space=pl.ANY),
                pl.BlockSpec((None, k, tn), lambda e, n, bs, nb: (e, 0, n)),
                pl.BlockSpec((None, 1, tn), lambda e, n, bs, nb: (e, 0, n))]
    args = [x, w, bias.reshape(n_e, 1, n_total)]
    if swiglu:
        sel = (lax.broadcasted_iota(jnp.int32, (2 * V7X_LANES, V7X_LANES), 0)
               == 2 * lax.broadcasted_iota(jnp.int32, (2 * V7X_LANES, V7X_LANES), 1)).astype(BF16)
        in_specs.append(pl.BlockSpec((2 * V7X_LANES, V7X_LANES), lambda e, n, bs, nb: (0, 0)))
        args.append(sel)
    return pl.pallas_call(
        functools.partial(_expert_mm_body, tm=tm, swiglu=swiglu, n_blocks=r_total // tm),
        grid_spec=pltpu.PrefetchScalarGridSpec(
            num_scalar_prefetch=2,
            grid=(n_e, n_total // tn),
            in_specs=in_specs,
            out_specs=pl.BlockSpec(memory_space=pl.ANY),
            scratch_shapes=[pltpu.VMEM((k, tn), BF16),
                            pltpu.VMEM((SUPER_BLOCKS * tm, k), BF16),
                            pltpu.VMEM((SUPER_BLOCKS * tm, out_tn), out_dtype),
                            pltpu.SemaphoreType.DMA((1,)),
                            pltpu.SemaphoreType.DMA((1,))]),
        out_shape=jax.ShapeDtypeStruct((r_total, out_n), out_dtype),
        compiler_params=_cparams(("arbitrary", "arbitrary")),
        name="moe_gate_up" if swiglu else "moe_down",
    )(bstart, nblk, *args)


def _combine_rows_copy(src_hbm, buf_ref, sem_ref, row, slot, k, r):
    return pltpu.make_async_copy(src_hbm.at[pl.ds(row, 1), :],
                                 buf_ref.at[slot, k, pl.ds(r, 1), :],
                                 sem_ref.at[slot])


def _combine_body(dest_ref, yb_hbm, x_ref, gate_ref, g_ref, o1_ref, o2_ref, buf_ref, sem_ref, *, tm,
                  n_first):
    i = pl.program_id(0)
    n = pl.num_programs(0)

    def issue(block, slot):
        def body(r, carry):
            for k in range(TOP_K):
                row = dest_ref[(block * tm + r) * TOP_K + k]
                _combine_rows_copy(yb_hbm, buf_ref, sem_ref, row, slot, k, r).start()
            return carry
        lax.fori_loop(0, tm, body, 0, unroll=2)

    @pl.when(i == 0)
    def _():
        issue(0, 0)

    @pl.when(i + 1 < n)
    def _():
        issue(i + 1, (i + 1) % 2)

    slot = i % 2
    for k in range(TOP_K):
        pltpu.make_async_copy(yb_hbm.at[pl.ds(0, tm), :], buf_ref.at[slot, k], sem_ref.at[slot]).wait()
    gates = gate_ref[...]
    acc = x_ref[...]
    for k in range(TOP_K):
        acc = acc + gates[:, k:k + 1] * buf_ref[slot, k]
    ms = jnp.mean(acc * acc, axis=-1, keepdims=True)
    res = acc * lax.rsqrt(ms + RMS_EPS) * g_ref[...]

    @pl.when(i < n_first)
    def _():
        o1_ref[...] = res

    @pl.when(i >= n_first)
    def _():
        o2_ref[...] = res


def _combine(yb, dest_flat, x, gates, g, tm, t_first):
    t, d = x.shape
    n_first = t_first // tm
    return pl.pallas_call(
        functools.partial(_combine_body, tm=tm, n_first=n_first),
        grid_spec=pltpu.PrefetchScalarGridSpec(
            num_scalar_prefetch=1,
            grid=(t // tm,),
            in_specs=[pl.BlockSpec(memory_space=pl.ANY),
                      pl.BlockSpec((tm, d), lambda i, dest: (i, 0)),
                      pl.BlockSpec((tm, V7X_LANES), lambda i, dest: (i, 0)),
                      pl.BlockSpec((1, d), lambda i, dest: (0, 0))],
            out_specs=[pl.BlockSpec((tm, d), lambda i, dest: (jnp.minimum(i, n_first - 1), 0)),
                       pl.BlockSpec((tm, d), lambda i, dest: (jnp.maximum(i - n_first, 0), 0))],
            scratch_shapes=[pltpu.VMEM((2, TOP_K, tm, d), F32), pltpu.SemaphoreType.DMA((2,))]),
        out_shape=[jax.ShapeDtypeStruct((t_first, d), F32),
                   jax.ShapeDtypeStruct((t - t_first, d), F32)],
        compiler_params=_cparams(("arbitrary",)),
        name="moe_combine",
    )(dest_flat, yb, x, gates, g.reshape(1, d))


def _routing_tables(top_idx, n_e, tm):
    t = top_idx.shape[0]
    p = t * TOP_K
    flat_e = top_idx.reshape(p)
    onehot = (flat_e[:, None] == jnp.arange(n_e, dtype=jnp.int32)[None, :]).astype(jnp.int32)
    rank = jnp.take_along_axis(jnp.cumsum(onehot, axis=0), flat_e[:, None], axis=1)[:, 0] - 1
    counts = jnp.sum(onehot, axis=0)
    pcounts = (counts + tm - 1) // tm * tm
    pends = jnp.cumsum(pcounts)
    pstarts = pends - pcounts
    dest = (pstarts[flat_e] + rank).astype(jnp.int32)
    n_blocks = p // tm + n_e
    flat_tok = jnp.arange(p, dtype=jnp.int32) // TOP_K
    slot_tok = jnp.zeros((n_blocks * tm,), jnp.int32).at[dest].set(flat_tok)
    return dest, slot_tok, (pstarts // tm).astype(jnp.int32), (pcounts // tm).astype(jnp.int32)


def kernel(x_prompt, x_sample, state_s5_re, state_s5_im, state_hgrn, norm_mix, w_in, b_gate, s5_lam_re, s5_lam_im, s5_log_step, s5_b_re, s5_b_im, s5_c_re, s5_c_im, s5_d, s5_w_glu, hgrn_lb_logits, hgrn_norm, w_branch_s5, w_branch_hgrn, w_out, norm_ffn, w_router, b_router, w_gate_up, b_gate_up, w_down, b_down, norm_final):
    depth = w_in.shape[0]
    assert depth == 1, "single-layer stack"
    n_p, s_p, d = x_prompt.shape
    n_s, s_s, _ = x_sample.shape
    t_p, t_s = n_p * s_p, n_s * s_s
    g_s5, p_s5 = s5_lam_re.shape[1:]
    d_s5 = g_s5 * S5_GROUP
    n_heads = state_hgrn.shape[2]
    d_h = n_heads * HGRN_HEAD_DIM
    n_e = w_router.shape[2]
    l = 0

    lb_all = jnp.cumsum(jax.nn.softmax(hgrn_lb_logits.astype(F32), axis=0), axis=0)[:depth]
    x = jnp.concatenate([x_prompt.reshape(t_p, d), x_sample.reshape(t_s, d)], axis=0)
    t = t_p + t_s

    h = _rmsnorm(x, norm_mix[l], BF16)
    z = _dense_mm([(h, 0)], [w_in[l]], [], lambda accs, ex: accs[0], F32, tm=1024, tn=512, name="in_proj")

    wb, wc, lam = _s5_params(s5_lam_re[l], s5_lam_im[l], s5_log_step[l], s5_b_re[l], s5_b_im[l],
                             s5_c_re[l], s5_c_im[l])
    d_flat = s5_d[l].reshape(1, d_s5)
    zeros_s5 = jnp.zeros((n_p, g_s5 * p_s5), F32)
    y_p, p_re, p_im = _s5_scan(z, 0, n_p, s_p, zeros_s5, zeros_s5, wb, wc, lam, d_flat, part_rows=256)
    y_s, s_re, s_im = _s5_scan(z, t_p, n_s, s_s, state_s5_re[l].reshape(n_s, -1),
                               state_s5_im[l].reshape(n_s, -1), wb, wc, lam, d_flat, part_rows=256)
    y = jnp.concatenate([y_p, y_s], axis=0)
    a = _dense_mm([(y, 0)], [s5_w_glu[l]],
                  [(y, (1024, 512), lambda n, m: (m, n))],
                  lambda accs, ex: ex[0] * _sigmoid(accs[0]), BF16, tm=1024, tn=512, name="s5_glu")

    lb_flat = lb_all[l].reshape(1, d_h)
    cb0 = d_s5 // d_h
    zeros_hg = jnp.zeros((n_p,) + state_hgrn.shape[2:], F32)
    o_p, p_hg = _hgrn(z, 0, n_p, s_p, zeros_hg, lb_flat, hgrn_norm[l], cb0)
    o_s, s_hg = _hgrn(z, t_p, n_s, s_s, state_hgrn[l], lb_flat, hgrn_norm[l], cb0)
    b = jnp.concatenate([o_p, o_s], axis=0)

    tn = 512
    ga_cb = (d_s5 + 4 * d_h) // tn
    gb_cb = ga_cb + d // tn
    bg = b_gate[l].reshape(1, 2 * d)
    merged = _dense_mm(
        [(a, 0), (b, 0)], [w_branch_s5[l], w_branch_hgrn[l]],
        [(z, (512, tn), lambda n, m: (m, ga_cb + n)),
         (z, (512, tn), lambda n, m: (m, gb_cb + n)),
         (bg, (1, tn), lambda n, m: (0, n)),
         (bg, (1, tn), lambda n, m: (0, d // tn + n))],
        lambda accs, ex: (_sigmoid(ex[0] + ex[2]) * accs[0] + _sigmoid(ex[1] + ex[3]) * accs[1]),
        BF16, tm=512, tn=tn, name="merge")
    x1 = _dense_mm([(merged, 0)], [w_out[l]], [(x, (1024, 512), lambda n, m: (m, n))],
                   lambda accs, ex: ex[0] + accs[0], F32, tm=1024, tn=512, name="out_proj")

    tm_e = 256
    xn, idx_wide, gate_wide = _router(x1, norm_ffn[l], w_router[l], b_router[l])
    top_idx = idx_wide[:, :TOP_K]
    dest, slot_tok, bstart, nblk = _routing_tables(top_idx, n_e, tm_e)
    xs = _gather_rows(xn, t, slot_tok, tm_e)
    act = _expert_mm(xs, w_gate_up[l], b_gate_up[l], bstart, nblk, tm_e, 512, swiglu=True)
    yb = _expert_mm(act, w_down[l], b_down[l], bstart, nblk, tm_e, 512, swiglu=False)
    y_p_out, y_s_out = _combine(yb, dest, x1, gate_wide, norm_final, tm=128, t_first=t_p)

    y_prompt = y_p_out.reshape(n_p, s_p, d)
    y_sample = y_s_out.reshape(n_s, s_s, d)
    return (y_prompt, y_sample,
            p_re.reshape(1, n_p, g_s5, p_s5), p_im.reshape(1, n_p, g_s5, p_s5), p_hg[None],
            s_re.reshape(1, n_s, g_s5, p_s5), s_im.reshape(1, n_s, g_s5, p_s5), s_hg[None])
```
